```python
import jax, jax.numpy as jnp
from jax import lax
import numpy as np

D_MODEL = 1024
BATCH = 16
SEQ = 4096
DEPTH = 4

H_A = 8
DH_A = 128
D_CQ = 256
ROT_A = DH_A // 4
H_IDX = 8
D_IDX = 64
ROT_IDX = D_IDX // 4
TOPK_MAX = 256
Q_BLOCK = 128
ROPE_THETA = 500000.0
MAX_POS_OFFSET = 1024
N_R = 64
H_R = D_MODEL // N_R
D_R = H_R * N_R
LORA_W = 64
LORA_A = 64
LORA_V = 32
LORA_G = 128
GN_EPS = 64e-5
D_FF = ((8 * D_MODEL // 3 + 127) // 128) * 128
CONV_W = 3
LN_EPS = 1e-5

ATT_SIZES = (D_CQ, DH_A, DH_A, D_IDX, H_IDX)
RWKV_SIZES = (D_R, D_R, D_R, LORA_W, LORA_A, LORA_G)
N_ATT = sum(ATT_SIZES)
N_GATE = 2 * D_MODEL
N_RWKV = sum(RWKV_SIZES)
N_IN0 = N_ATT + N_GATE + N_RWKV
N_IN = N_IN0 + LORA_V

kernel_name = 'hybrid_dsa_rwkv7_deepnorm_trunk'


def split_cols(t, sizes):
    offs = np.cumsum(sizes)[:-1].tolist()
    return jnp.split(t, offs, axis=-1)


def layer_norm(x, g, b, eps=LN_EPS):
    xf = x.astype(jnp.float32)
    mu = jnp.mean(xf, -1, keepdims=True)
    var = jnp.mean(jnp.square(xf - mu), -1, keepdims=True)
    return ((xf - mu) * lax.rsqrt(var + eps) * g + b).astype(x.dtype)


def rms_norm(x, g, eps=1e-6):
    xf = x.astype(jnp.float32)
    return (xf * lax.rsqrt(jnp.mean(jnp.square(xf), -1, keepdims=True) + eps) * g).astype(x.dtype)


def rope_tables(positions, rot_dim):
    inv_freq = jnp.power(jnp.float32(ROPE_THETA), -jnp.arange(0, rot_dim, 2, dtype=jnp.float32) / rot_dim)
    ang = positions.astype(jnp.float32)[..., None] * inv_freq
    return jnp.cos(ang)[:, :, None, :], jnp.sin(ang)[:, :, None, :]


def partial_rope(x, cos, sin):
    half = cos.shape[-1]
    xf = x.astype(jnp.float32)
    x1, x2, rest = xf[..., :half], xf[..., half:2 * half], xf[..., 2 * half:]
    return jnp.concatenate([x1 * cos - x2 * sin, x2 * cos + x1 * sin, rest], -1).astype(x.dtype)


def token_shift(p, mu):
    prev = jnp.pad(p, ((0, 0), (1, 0), (0, 0)))[:, :-1]
    return p + (prev - p) * mu


def dsa_sparse_mqa(cq_raw, k_raw, v_raw, ik_raw, iw_raw, g_cq, w_uq, w_iq, g_ik, b_ik, w_oa, rope_a, rope_i):
    B, S, _ = cq_raw.shape
    topk = min(TOPK_MAX, S // 4)
    nb = S // Q_BLOCK
    c_q = rms_norm(cq_raw, g_cq)
    q = partial_rope((c_q @ w_uq).reshape(B, S, H_A, DH_A), *rope_a)
    k = partial_rope(k_raw[:, :, None, :], *rope_a)[:, :, 0]
    v = v_raw
    qi = partial_rope((c_q @ w_iq).reshape(B, S, H_IDX, D_IDX), *rope_i)
    ki = partial_rope(layer_norm(ik_raw, g_ik, b_ik)[:, :, None, :], *rope_i)[:, :, 0]
    wi = iw_raw.astype(jnp.float32) * (H_IDX ** -0.5 * D_IDX ** -0.5)
    key_pos = jnp.arange(S, dtype=jnp.int32)

    def to_blocks(t):
        return t.reshape(B, nb, Q_BLOCK, *t.shape[2:]).swapaxes(0, 1)

    gather = jax.vmap(lambda t, i: t[i])

    def block(args):
        qb, qib, wib, tb = args
        rel = jax.nn.relu(jnp.einsum('bqhd,bsd->bqhs', qib, ki).astype(jnp.float32))
        score = jnp.einsum('bqhs,bqh->bqs', rel, wib)
        score = jnp.where((key_pos[None, :] <= tb[:, None])[None], score, -jnp.inf)
        _, idx = lax.top_k(score, topk)
        valid = idx <= tb[None, :, None]
        k_sel = gather(k, idx)
        v_sel = gather(v, idx)
        logits = jnp.einsum('bqhd,bqkd->bqhk', qb, k_sel).astype(jnp.float32) * DH_A ** -0.5
        logits = jnp.where(valid[:, :, None, :], logits, -jnp.inf)
        probs = jax.nn.softmax(logits, axis=-1).astype(v.dtype)
        return jnp.einsum('bqhk,bqkd->bqhd', probs, v_sel)

    o = lax.map(block, (to_blocks(q), to_blocks(qi), to_blocks(wi), key_pos.reshape(nb, Q_BLOCK)))
    o = o.swapaxes(0, 1).reshape(B, S, H_A * DH_A)
    return o @ w_oa


def rwkv7_time_mix(r, k, v, p_w, p_a, p_g, w0, w2, a0, a2, g2, k_k, k_a, r_k, g_lnx, b_lnx, w_ob):
    B, S, _ = r.shape
    w = -jax.nn.softplus(-(w0 + jnp.tanh(p_w) @ w2)) - 0.5
    decay = jnp.exp(-jnp.exp(w.astype(jnp.float32)))
    a = jax.nn.sigmoid(a0 + p_a @ a2)
    g = jax.nn.sigmoid(p_g) @ g2
    hd = lambda t: t.reshape(B, S, H_R, N_R).astype(jnp.float32)
    kk = hd(k * k_k)
    kk = kk / jnp.maximum(jnp.linalg.norm(kk, axis=-1, keepdims=True), 1e-12)
    k = k * (1 + (a - 1) * k_a)
    r_h, k_h, v_h, a_h, w_h = hd(r), hd(k), hd(v), hd(a), hd(decay)
    b_h = kk * a_h

    def step(state, inp):
        r_t, w_t, k_t, v_t, kk_t, b_t = inp
        sa = jnp.einsum('bhvk,bhk->bhv', state, -kk_t)
        state = state * w_t[:, :, None, :] + sa[..., None] * b_t[:, :, None, :] + v_t[..., None] * k_t[:, :, None, :]
        return state, jnp.einsum('bhvk,bhk->bhv', state, r_t)

    xs = tuple(t.swapaxes(0, 1) for t in (r_h, w_h, k_h, v_h, kk, b_h))
    state0 = jnp.zeros((B, H_R, N_R, N_R), jnp.float32)
    _, y = lax.scan(step, state0, xs)
    y = y.swapaxes(0, 1)
    mu = jnp.mean(y, -1, keepdims=True)
    var = jnp.mean(jnp.square(y - mu), -1, keepdims=True)
    y = ((y - mu) * lax.rsqrt(var + GN_EPS)).reshape(B, S, D_R) * g_lnx + b_lnx
    bonus = jnp.sum(r_h * k_h * r_k.astype(jnp.float32), -1, keepdims=True) * v_h
    y = y + bonus.reshape(B, S, D_R)
    return (y * g).astype(r.dtype) @ w_ob


def conv_glu_ffn(x, w_up, conv_w, conv_b, w_down):
    h_gate, h_val = jnp.split(x @ w_up, 2, axis=-1)
    h_gate = lax.conv_general_dilated(h_gate, conv_w[:, None, :], (1,), ((CONV_W - 1, 0),),
                                      dimension_numbers=('NWC', 'WIO', 'NWC'),
                                      feature_group_count=D_FF) + conv_b
    return (jax.nn.silu(h_gate) * h_val) @ w_down


def setup_inputs(seed: int = 0) -> dict:
    key = jax.random.key(seed)
    ks = iter(jax.random.split(key, 48))
    nrm = lambda shape, scale: jax.random.normal(next(ks), shape, jnp.float32) * scale
    L = DEPTH
    beta = (8 * DEPTH) ** -0.25
    x = nrm((BATCH, SEQ, D_MODEL), 1.0)
    offset = jax.random.randint(next(ks), (BATCH, 1), 0, MAX_POS_OFFSET, dtype=jnp.int32)
    positions = offset + jnp.arange(SEQ, dtype=jnp.int32)[None, :]
    return {
        'x': x,
        'positions': positions,
        'w_in0': nrm((D_MODEL, N_IN0), D_MODEL ** -0.5),
        'w_in_rest': nrm((L - 1, D_MODEL, N_IN), D_MODEL ** -0.5),
        'b_gate': nrm((L, N_GATE), 0.02),
        'g_cq': 1.0 + nrm((L, D_CQ), 0.02),
        'w_uq': nrm((L, D_CQ, H_A * DH_A), D_CQ ** -0.5),
        'w_iq': nrm((L, D_CQ, H_IDX * D_IDX), D_CQ ** -0.5),
        'g_ik': 1.0 + nrm((L, D_IDX), 0.02),
        'b_ik': nrm((L, D_IDX), 0.02),
        'w_oa': nrm((L, H_A * DH_A, D_MODEL), beta * (H_A * DH_A) ** -0.5),
        'mu_rwkv': jax.random.uniform(next(ks), (L, N_RWKV), jnp.float32),
        'mu_vres': jax.random.uniform(next(ks), (L - 1, LORA_V), jnp.float32),
        'w0': jnp.linspace(-6.5, -1.5, D_R, dtype=jnp.float32)[None, :] + nrm((L, D_R), 0.1),
        'w2': nrm((L, LORA_W, D_R), 0.5 * LORA_W ** -0.5),
        'a0': nrm((L, D_R), 0.1),
        'a2': nrm((L, LORA_A, D_R), LORA_A ** -0.5),
        'v0': 0.5 + nrm((L - 1, D_R), 0.1),
        'v2': nrm((L - 1, LORA_V, D_R), LORA_V ** -0.5),
        'g2': nrm((L, LORA_G, D_R), LORA_G ** -0.5),
        'k_k': 0.85 + nrm((L, D_R), 0.05),
        'k_a': 1.0 + nrm((L, D_R), 0.05),
        'r_k': nrm((L, H_R, N_R), 0.1),
        'g_lnx': 1.0 + nrm((L, D_R), 0.02),
        'b_lnx': nrm((L, D_R), 0.02),
        'w_ob': nrm((L, D_R, D_MODEL), beta * D_R ** -0.5),
        'w_out': nrm((L, D_MODEL, D_MODEL), beta * D_MODEL ** -0.5),
        'ln1_g': 1.0 + nrm((L, D_MODEL), 0.02),
        'ln1_b': nrm((L, D_MODEL), 0.02),
        'w_up': nrm((L, D_MODEL, 2 * D_FF), D_MODEL ** -0.5),
        'conv_w': nrm((L, CONV_W, D_FF), CONV_W ** -0.5),
        'conv_b': nrm((L, D_FF), 0.02),
        'w_down': nrm((L, D_FF, D_MODEL), beta * D_FF ** -0.5),
        'ln2_g': 1.0 + nrm((L, D_MODEL), 0.02),
        'ln2_b': nrm((L, D_MODEL), 0.02),
    }


def reference(x, positions, w_in0, w_in_rest, b_gate, g_cq, w_uq, w_iq, g_ik, b_ik, w_oa,
              mu_rwkv, mu_vres, w0, w2, a0, a2, v0, v2, g2, k_k, k_a, r_k, g_lnx, b_lnx, w_ob,
              w_out, ln1_g, ln1_b, w_up, conv_w, conv_b, w_down, ln2_g, ln2_b):
    alpha = (2 * DEPTH) ** 0.25
    rope_a = rope_tables(positions, ROT_A)
    rope_i = rope_tables(positions, ROT_IDX)
    v_first = None
    for i in range(DEPTH):
        p = x @ (w_in0 if i == 0 else w_in_rest[i - 1])
        p_att = p[..., :N_ATT]
        p_gate = p[..., N_ATT:N_ATT + N_GATE]
        p_rw = token_shift(p[..., N_ATT + N_GATE:N_IN0], mu_rwkv[i])
        cq_raw, k_raw, v_raw, ik_raw, iw_raw = split_cols(p_att, ATT_SIZES)
        y_a = dsa_sparse_mqa(cq_raw, k_raw, v_raw, ik_raw, iw_raw, g_cq[i], w_uq[i], w_iq[i],
                             g_ik[i], b_ik[i], w_oa[i], rope_a, rope_i)
        pr, pk, pv, pw, pa, pg = split_cols(p_rw, RWKV_SIZES)
        if i == 0:
            v_first = pv
            v = pv
        else:
            p_vr = token_shift(p[..., N_IN0:], mu_vres[i - 1])
            v = pv + (v_first - pv) * jax.nn.sigmoid(v0[i - 1] + p_vr @ v2[i - 1])
        y_b = rwkv7_time_mix(pr, pk, v, pw, pa, pg, w0[i], w2[i], a0[i], a2[i], g2[i],
                             k_k[i], k_a[i], r_k[i], g_lnx[i], b_lnx[i], w_ob[i])
        g_a, g_b = jnp.split(jax.nn.sigmoid(p_gate + b_gate[i]), 2, axis=-1)
        x = layer_norm(alpha * x + (g_a * y_a + g_b * y_b) @ w_out[i], ln1_g[i], ln1_b[i])
        x = layer_norm(alpha * x + conv_glu_ffn(x, w_up[i], conv_w[i], conv_b[i], w_down[i]), ln2_g[i], ln2_b[i])
    return x
```

```python
import functools

import jax
import jax.numpy as jnp
import numpy as np
from jax import lax
from jax.experimental import pallas as pl
from jax.experimental.pallas import tpu as pltpu

D_MODEL = 1024
DEPTH = 4
H_A = 8
DH_A = 128
D_CQ = 256
ROT_A = DH_A // 4
H_IDX = 8
D_IDX = 64
ROT_IDX = D_IDX // 4
TOPK_MAX = 256
ROPE_THETA = 500000.0
N_R = 64
H_R = D_MODEL // N_R
D_R = H_R * N_R
LORA_W = 64
LORA_A = 64
LORA_V = 32
LORA_G = 128
GN_EPS = 64e-5
D_FF = ((8 * D_MODEL // 3 + 127) // 128) * 128
LN_EPS = 1e-5

N_ATT = D_CQ + 2 * DH_A + D_IDX + H_IDX
N_GATE = 2 * D_MODEL
N_RWKV = 3 * D_R + LORA_W + LORA_A + LORA_G
N_IN0 = N_ATT + N_GATE + N_RWKV

LANES = 128
VMEM_LIMIT_BYTES = 48 * 1024 * 1024

N_ATT_PAD = 640
OFF_GATE = N_ATT_PAD
OFF_RW = OFF_GATE + N_GATE
OFF_VR = OFF_RW + N_RWKV
N_IN_PAD = 6144

ATTN_TILE = 256
SCAN_CHUNK = 64
SCAN_HEADS = 2
NEG_BIG = -1e30
INT_MIN = -(2 ** 31)


def _mm_kernel(x_ref, w_ref, o_ref):
    o_ref[...] = jnp.dot(x_ref[...].astype(jnp.bfloat16), w_ref[...],
                         preferred_element_type=jnp.float32).astype(o_ref.dtype)


def _pick_tile(n, cap):
    best = None
    for t in range(LANES, min(n, cap) + 1, LANES):
        if n % t == 0:
            best = t
    return best if best is not None else n


def _mm(x, w, out_dtype=jnp.float32, tm=None):
    m, k = x.shape
    n = w.shape[1]
    w = w.astype(jnp.bfloat16)
    if tm is None:
        tm = 1024 if k <= 1024 else 512
    tm = min(tm, m)
    tn = _pick_tile(n, 1024 if k <= 1024 else 512)
    return pl.pallas_call(
        _mm_kernel,
        grid=(m // tm, n // tn),
        in_specs=[pl.BlockSpec((tm, k), lambda i, j: (i, 0)),
                  pl.BlockSpec((k, tn), lambda i, j: (0, j))],
        out_specs=pl.BlockSpec((tm, tn), lambda i, j: (i, j)),
        out_shape=jax.ShapeDtypeStruct((m, n), out_dtype),
        compiler_params=pltpu.CompilerParams(
            dimension_semantics=("parallel", "arbitrary"),
            vmem_limit_bytes=VMEM_LIMIT_BYTES),
        name="mm",
    )(x, w)


def _sortable_key(score):
    score = jnp.where(score == 0.0, 0.0, score)
    bits = pltpu.bitcast(score, jnp.int32)
    return jnp.where(bits < 0, bits ^ jnp.int32(0x7FFFFFFF), bits)


def _attn_kernel(q_ref, qi_ref, wi_ref, k_ref, v_ref, ki_ref, o_ref,
                 key_ref, thr_ref, jthr_ref, m_ref, l_ref, acc_ref, *, t, topk, scale):
    qb = pl.program_id(1)
    n_kt = qb + 1
    nh = q_ref.shape[1]
    row = lax.broadcasted_iota(jnp.int32, (t, t), 0)
    col = lax.broadcasted_iota(jnp.int32, (t, t), 1)

    qi = qi_ref[0].reshape(nh * t, qi_ref.shape[3])
    wi = wi_ref[0]

    def score_tile(kt, carry):
        ki_t = ki_ref[0, pl.ds(kt * t, t), :]
        rel = lax.dot_general(qi, ki_t, (((1,), (1,)), ((), ())),
                              preferred_element_type=jnp.float32)
        rel = jnp.maximum(rel, 0.0).reshape(nh, t, t)
        score = rel[0] * wi[:, 0:1]
        for h in range(1, nh):
            score = score + rel[h] * wi[:, h:h + 1]
        score = jnp.where(jnp.logical_and(kt == qb, col > row), -jnp.inf, score)
        key_ref[kt] = _sortable_key(score)
        return carry

    lax.fori_loop(0, n_kt, score_tile, 0)

    def count(pred_fn):
        def body(kt, acc):
            c = jnp.where(pred_fn(key_ref[kt], kt), 1, 0).astype(jnp.int32)
            part = c[:, 0:LANES]
            for j in range(1, t // LANES):
                part = part + c[:, j * LANES:(j + 1) * LANES]
            return acc + part
        acc = lax.fori_loop(0, n_kt, body, jnp.zeros((t, LANES), jnp.int32))
        return jnp.sum(acc, axis=1, keepdims=True)

    def count_ge(cand):
        return count(lambda kk, kt: kk >= cand)

    thr0 = jnp.where(count_ge(jnp.zeros((t, 1), jnp.int32)) >= topk,
                     jnp.int32(0), jnp.int32(INT_MIN))

    def bit_step(i, thr):
        cand = thr | jnp.left_shift(jnp.int32(1), 30 - i)
        return jnp.where(count_ge(cand) >= topk, cand, thr)

    thr = lax.fori_loop(0, 31, bit_step, thr0)
    thr_ref[...] = thr
    jthr_ref[...] = jnp.full((t, 1), 2 ** 30, jnp.int32)
    n_ge = count_ge(thr)

    @pl.when(jnp.max(n_ge) > topk)
    def _():
        thr_t = thr_ref[...]
        need = topk - count(lambda kk, kt: kk > thr_t)

        idx_bits = int(np.ceil(np.log2(key_ref.shape[0] * t)))

        def idx_step(i, j):
            cand = j | jnp.left_shift(jnp.int32(1), idx_bits - 1 - i)
            below = count(lambda kk, kt: jnp.logical_and(kk == thr_t, col + kt * t < cand))
            return jnp.where(below < need, cand, j)

        jthr_ref[...] = lax.fori_loop(0, idx_bits, idx_step, jnp.zeros((t, 1), jnp.int32))

    q = q_ref[0].reshape(nh * t, q_ref.shape[3])
    m_ref[...] = jnp.full(m_ref.shape, NEG_BIG, jnp.float32)
    l_ref[...] = jnp.zeros(l_ref.shape, jnp.float32)
    acc_ref[...] = jnp.zeros(acc_ref.shape, jnp.float32)
    thr = thr_ref[...]
    jthr = jthr_ref[...]

    def attn_tile(kt, carry):
        kk = key_ref[kt]
        gcol = col + kt * t
        sel = jnp.logical_or(kk > thr, jnp.logical_and(kk == thr, gcol <= jthr))
        sel = jnp.logical_and(sel, gcol <= row + qb * t)
        bias = jnp.where(sel, 0.0, NEG_BIG)
        k_t = k_ref[0, pl.ds(kt * t, t), :]
        v_t = v_ref[0, pl.ds(kt * t, t), :]
        s = lax.dot_general(q, k_t, (((1,), (1,)), ((), ())),
                            preferred_element_type=jnp.float32) * scale
        s = (s.reshape(nh, t, t) + bias[None]).reshape(nh * t, t)
        m_old = m_ref[...]
        m_new = jnp.maximum(m_old, jnp.max(s, axis=1, keepdims=True))
        alpha = jnp.exp(m_old - m_new)
        p = jnp.exp(s - m_new)
        l_ref[...] = l_ref[...] * alpha + jnp.sum(p, axis=1, keepdims=True)
        acc_ref[...] = acc_ref[...] * alpha + jnp.dot(
            p.astype(jnp.bfloat16), v_t, preferred_element_type=jnp.float32)
        m_ref[...] = m_new
        return carry

    lax.fori_loop(0, n_kt, attn_tile, 0)
    out = acc_ref[...] / l_ref[...]
    dh = q_ref.shape[3]
    for h in range(nh):
        o_ref[0, :, h * dh:(h + 1) * dh] = out[h * t:(h + 1) * t].astype(o_ref.dtype)


def _attention(q, qi, wi, k, v, ki, topk):
    b, nh, s, dh = q.shape
    di = qi.shape[3]
    t = min(ATTN_TILE, s)
    kern = functools.partial(_attn_kernel, t=t, topk=topk, scale=float(dh) ** -0.5)
    return pl.pallas_call(
        kern,
        grid=(b, s // t),
        in_specs=[
            pl.BlockSpec((1, nh, t, dh), lambda bi, qb: (bi, 0, qb, 0)),
            pl.BlockSpec((1, nh, t, di), lambda bi, qb: (bi, 0, qb, 0)),
            pl.BlockSpec((1, t, nh), lambda bi, qb: (bi, qb, 0)),
            pl.BlockSpec((1, s, dh), lambda bi, qb: (bi, 0, 0)),
            pl.BlockSpec((1, s, dh), lambda bi, qb: (bi, 0, 0)),
            pl.BlockSpec((1, s, di), lambda bi, qb: (bi, 0, 0)),
        ],
        out_specs=pl.BlockSpec((1, t, nh * dh), lambda bi, qb: (bi, qb, 0)),
        out_shape=jax.ShapeDtypeStruct((b, s, nh * dh), jnp.bfloat16),
        scratch_shapes=[
            pltpu.VMEM((s // t, t, t), jnp.int32),
            pltpu.VMEM((t, 1), jnp.int32),
            pltpu.VMEM((t, 1), jnp.int32),
            pltpu.VMEM((nh * t, 1), jnp.float32),
            pltpu.VMEM((nh * t, 1), jnp.float32),
            pltpu.VMEM((nh * t, dh), jnp.float32),
        ],
        compiler_params=pltpu.CompilerParams(
            dimension_semantics=("parallel", "arbitrary"),
            vmem_limit_bytes=VMEM_LIMIT_BYTES),
        name="dsa_attention",
    )(q, qi, wi, k, v, ki)


def _bdot(a, b, dims=(((1,), (0,)), ((), ()))):
    return lax.dot_general(a.astype(jnp.bfloat16), b.astype(jnp.bfloat16), dims,
                           preferred_element_type=jnp.float32)


_NT = (((1,), (1,)), ((), ()))
_TN = (((0,), (0,)), ((), ()))


def _scan_kernel(r_ref, lw_ref, k_ref, v_ref, kk_ref, b_ref, y_ref, s_ref, *, chunk, heads):
    @pl.when(pl.program_id(2) == 0)
    def _():
        s_ref[...] = jnp.zeros(s_ref.shape, jnp.float32)

    n = r_ref.shape[3]
    row = lax.broadcasted_iota(jnp.int32, (chunk, chunk), 0)
    col = lax.broadcasted_iota(jnp.int32, (chunk, chunk), 1)
    incl = row >= col
    strict = row > col
    tril = jnp.where(incl, 1.0, 0.0).astype(jnp.float32)

    for h in range(heads):
        r = r_ref[0, h]
        lw = lw_ref[0, h]
        k = k_ref[0, h]
        v = v_ref[0, h]
        kk = kk_ref[0, h]
        b = b_ref[0, h]
        s0 = s_ref[h]

        c = jnp.dot(tril, lw, precision=lax.Precision.HIGHEST,
                    preferred_element_type=jnp.float32)
        e_neg = jnp.exp(-c)
        at = -kk * jnp.exp(c - lw)
        rt = r * jnp.exp(c)
        bt = b * e_neg
        kt = k * e_neg
        p_last = jnp.exp(c[chunk - 1:chunk, :])

        a = _bdot(jnp.concatenate([at, rt], axis=0), jnp.concatenate([bt, kt], axis=0), _NT)
        nmat = jnp.where(strict, a[:chunk, :chunk], 0.0)
        a_ak = jnp.where(strict, a[:chunk, chunk:], 0.0)
        a_rb = jnp.where(incl, a[chunk:, :chunk], 0.0)
        a_rk = jnp.where(incl, a[chunk:, chunk:], 0.0)

        x = jnp.concatenate([at, _bdot(a_ak, v)], axis=1)
        mpow = nmat
        steps = max(1, int(np.ceil(np.log2(chunk))))
        for i in range(steps):
            x = x + _bdot(mpow, x)
            if i + 1 < steps:
                mpow = _bdot(mpow, mpow)
        ah = x[:, :n]
        w = x[:, n:]

        bp = bt * p_last
        kp = kt * p_last
        g = _bdot(ah, bp, _TN)
        hmat = _bdot(w, bp, _TN) + _bdot(v, kp, _TN)
        qh = rt + _bdot(a_rb, ah)
        y0 = _bdot(a_rb, w) + _bdot(a_rk, v)

        y_ref[0, h] = y0 + _bdot(qh, s0, _NT)
        s_ref[h] = s0 * p_last + _bdot(s0, g) + hmat


def _rwkv_scan(r, lw, k, v, kk, b):
    bsz, nh, s, n = r.shape
    chunk = min(SCAN_CHUNK, s)
    heads = SCAN_HEADS
    spec = pl.BlockSpec((1, heads, chunk, n), lambda bi, hi, ci: (bi, hi, ci, 0))
    kern = functools.partial(_scan_kernel, chunk=chunk, heads=heads)
    return pl.pallas_call(
        kern,
        grid=(bsz, nh // heads, s // chunk),
        in_specs=[spec] * 6,
        out_specs=spec,
        out_shape=jax.ShapeDtypeStruct((bsz, nh, s, n), jnp.float32),
        scratch_shapes=[pltpu.VMEM((heads, n, n), jnp.float32)],
        compiler_params=pltpu.CompilerParams(
            dimension_semantics=("parallel", "parallel", "arbitrary"),
            vmem_limit_bytes=VMEM_LIMIT_BYTES),
        name="rwkv7_scan",
    )(r, lw, k, v, kk, b)


def _layer_norm(x, g, b, eps=LN_EPS):
    mu = jnp.mean(x, -1, keepdims=True)
    var = jnp.mean(jnp.square(x - mu), -1, keepdims=True)
    return (x - mu) * lax.rsqrt(var + eps) * g + b


def _rms_norm(x, g, eps=1e-6):
    return x * lax.rsqrt(jnp.mean(jnp.square(x), -1, keepdims=True) + eps) * g


def _rope_tables(positions, rot_dim):
    inv_freq = jnp.power(jnp.float32(ROPE_THETA), -jnp.arange(0, rot_dim, 2, dtype=jnp.float32) / rot_dim)
    ang = positions.astype(jnp.float32)[..., None] * inv_freq
    return jnp.cos(ang)[:, :, None, :], jnp.sin(ang)[:, :, None, :]


def _partial_rope(x, cos, sin):
    half = cos.shape[-1]
    x1, x2, rest = x[..., :half], x[..., half:2 * half], x[..., 2 * half:]
    return jnp.concatenate([x1 * cos - x2 * sin, x2 * cos + x1 * sin, rest], -1)


def _token_shift(p, mu):
    prev = jnp.pad(p, ((0, 0), (1, 0), (0, 0)))[:, :-1]
    return p + (prev - p) * mu


def _pad_w_in(w, has_vr):
    k = w.shape[0]
    z = lambda n: jnp.zeros((k, n), w.dtype)
    parts = [w[:, :N_ATT], z(N_ATT_PAD - N_ATT), w[:, N_ATT:N_IN0]]
    used = N_ATT_PAD + N_GATE + N_RWKV
    if has_vr:
        parts.append(w[:, N_IN0:])
        used += LORA_V
    parts.append(z(N_IN_PAD - used))
    return jnp.concatenate(parts, axis=1).astype(jnp.bfloat16)


def kernel(x, positions, w_in0, w_in_rest, b_gate, g_cq, w_uq, w_iq, g_ik, b_ik, w_oa, mu_rwkv, mu_vres, w0, w2, a0, a2, v0, v2, g2, k_k, k_a, r_k, g_lnx, b_lnx, w_ob, w_out, ln1_g, ln1_b, w_up, conv_w, conv_b, w_down, ln2_g, ln2_b):
    bsz, seq, d = x.shape
    m = bsz * seq
    alpha = (2 * DEPTH) ** 0.25
    topk = min(TOPK_MAX, seq // 4)
    rope_a = _rope_tables(positions, ROT_A)
    rope_i = _rope_tables(positions, ROT_IDX)
    v_first = None
    mm3 = lambda t, w: _mm(t.reshape(m, t.shape[-1]), w).reshape(bsz, seq, w.shape[1])

    for i in range(DEPTH):
        w_in = _pad_w_in(w_in0 if i == 0 else w_in_rest[i - 1], i > 0)
        p = mm3(x, w_in)

        cq_raw = p[..., :D_CQ]
        k_raw = p[..., D_CQ:D_CQ + DH_A]
        v_raw = p[..., D_CQ + DH_A:D_CQ + 2 * DH_A]
        ik_raw = p[..., D_CQ + 2 * DH_A:D_CQ + 2 * DH_A + D_IDX]
        iw_raw = p[..., D_CQ + 2 * DH_A + D_IDX:N_ATT]
        c_q = _rms_norm(cq_raw, g_cq[i])
        q = _partial_rope(mm3(c_q, w_uq[i]).reshape(bsz, seq, H_A, DH_A), *rope_a)
        k_att = _partial_rope(k_raw[:, :, None, :], *rope_a)[:, :, 0]
        qi = _partial_rope(mm3(c_q, w_iq[i]).reshape(bsz, seq, H_IDX, D_IDX), *rope_i)
        ki = _partial_rope(_layer_norm(ik_raw, g_ik[i], b_ik[i])[:, :, None, :], *rope_i)[:, :, 0]
        wi = iw_raw * (H_IDX ** -0.5 * D_IDX ** -0.5)
        o = _attention(q.swapaxes(1, 2).astype(jnp.bfloat16), qi.swapaxes(1, 2).astype(jnp.bfloat16), wi,
                       k_att.astype(jnp.bfloat16), v_raw.astype(jnp.bfloat16), ki.astype(jnp.bfloat16), topk)
        y_a = mm3(o, w_oa[i])

        p_rw = _token_shift(p[..., OFF_RW:OFF_RW + N_RWKV], mu_rwkv[i])
        pr = p_rw[..., :D_R]
        pk = p_rw[..., D_R:2 * D_R]
        pv = p_rw[..., 2 * D_R:3 * D_R]
        pw = p_rw[..., 3 * D_R:3 * D_R + LORA_W]
        pa = p_rw[..., 3 * D_R + LORA_W:3 * D_R + LORA_W + LORA_A]
        pg = p_rw[..., 3 * D_R + LORA_W + LORA_A:]
        if i == 0:
            v_first = pv
            v_r = pv
        else:
            p_vr = _token_shift(p[..., OFF_VR:OFF_VR + LORA_V], mu_vres[i - 1])
            v_r = pv + (v_first - pv) * jax.nn.sigmoid(v0[i - 1] + mm3(p_vr, v2[i - 1]))
        w_dec = -jax.nn.softplus(-(w0[i] + mm3(jnp.tanh(pw), w2[i]))) - 0.5
        lw = -jnp.exp(w_dec)
        a_r = jax.nn.sigmoid(a0[i] + mm3(pa, a2[i]))
        g_r = mm3(jax.nn.sigmoid(pg), g2[i])
        hd = lambda t_: t_.reshape(bsz, seq, H_R, N_R)
        kk = hd(pk * k_k[i])
        kk = kk / jnp.maximum(jnp.sqrt(jnp.sum(kk * kk, -1, keepdims=True)), 1e-12)
        k_r = pk * (1 + (a_r - 1) * k_a[i])
        r_h, k_h, v_h, a_h, lw_h = hd(pr), hd(k_r), hd(v_r), hd(a_r), hd(lw)
        b_h = kk * a_h
        hm = lambda t_: t_.swapaxes(1, 2)
        y = _rwkv_scan(hm(r_h), hm(lw_h), hm(k_h), hm(v_h), hm(kk), hm(b_h)).swapaxes(1, 2)
        mu_y = jnp.mean(y, -1, keepdims=True)
        var_y = jnp.mean(jnp.square(y - mu_y), -1, keepdims=True)
        y = ((y - mu_y) * lax.rsqrt(var_y + GN_EPS)).reshape(bsz, seq, D_R) * g_lnx[i] + b_lnx[i]
        bonus = jnp.sum(r_h * k_h * r_k[i], -1, keepdims=True) * v_h
        y = y + bonus.reshape(bsz, seq, D_R)
        y_b = mm3(y * g_r, w_ob[i])

        gates = jax.nn.sigmoid(p[..., OFF_GATE:OFF_GATE + N_GATE] + b_gate[i])
        z = gates[..., :D_MODEL] * y_a + gates[..., D_MODEL:] * y_b
        x = _layer_norm(alpha * x + mm3(z, w_out[i]), ln1_g[i], ln1_b[i])

        hcat = mm3(x, w_up[i])
        h_gate, h_val = hcat[..., :D_FF], hcat[..., D_FF:]
        gp = jnp.pad(h_gate, ((0, 0), (2, 0), (0, 0)))
        conv = gp[:, :-2] * conv_w[i, 0] + gp[:, 1:-1] * conv_w[i, 1] + gp[:, 2:] * conv_w[i, 2] + conv_b[i]
        act = jax.nn.silu(conv) * h_val
        x = _layer_norm(alpha * x + mm3(act, w_down[i]), ln2_g[i], ln2_b[i])
    return x
```

```python
import functools

import jax
import jax.numpy as jnp
import numpy as np
from jax import lax
from jax.experimental import pallas as pl
from jax.experimental.pallas import tpu as pltpu

D_MODEL = 1024
DEPTH = 4
H_A = 8
DH_A = 128
D_CQ = 256
ROT_A = DH_A // 4
H_IDX = 8
D_IDX = 64
ROT_IDX = D_IDX // 4
TOPK_MAX = 256
ROPE_THETA = 500000.0
N_R = 64
H_R = D_MODEL // N_R
D_R = H_R * N_R
LORA_W = 64
LORA_A = 64
LORA_V = 32
LORA_G = 128
GN_EPS = 64e-5
D_FF = ((8 * D_MODEL // 3 + 127) // 128) * 128
LN_EPS = 1e-5

N_ATT = D_CQ + 2 * DH_A + D_IDX + H_IDX
N_GATE = 2 * D_MODEL
N_RWKV = 3 * D_R + LORA_W + LORA_A + LORA_G
N_IN0 = N_ATT + N_GATE + N_RWKV

LANES = 128
VMEM_LIMIT_BYTES = 48 * 1024 * 1024

N_ATT_PAD = 640
OFF_GATE = N_ATT_PAD
OFF_RW = OFF_GATE + N_GATE
OFF_VR = OFF_RW + N_RWKV
N_IN_PAD = 6144

ATTN_TILE = 256
SCAN_CHUNK = 64
NEG_BIG = -1e30
INT_MIN = -(2 ** 31)


def _mm_kernel(x_ref, w_ref, o_ref):
    o_ref[...] = jnp.dot(x_ref[...].astype(jnp.bfloat16), w_ref[...],
                         preferred_element_type=jnp.float32).astype(o_ref.dtype)


def _pick_tile(n, cap):
    best = None
    for t in range(LANES, min(n, cap) + 1, LANES):
        if n % t == 0:
            best = t
    return best if best is not None else n


def _mm(x, w, out_dtype=jnp.float32, tm=None):
    m, k = x.shape
    n = w.shape[1]
    w = w.astype(jnp.bfloat16)
    if tm is None:
        tm = 1024 if k <= 1024 else 512
    tm = min(tm, m)
    tn = _pick_tile(n, 1024 if k <= 1024 else 512)
    return pl.pallas_call(
        _mm_kernel,
        grid=(m // tm, n // tn),
        in_specs=[pl.BlockSpec((tm, k), lambda i, j: (i, 0)),
                  pl.BlockSpec((k, tn), lambda i, j: (0, j))],
        out_specs=pl.BlockSpec((tm, tn), lambda i, j: (i, j)),
        out_shape=jax.ShapeDtypeStruct((m, n), out_dtype),
        compiler_params=pltpu.CompilerParams(
            dimension_semantics=("parallel", "arbitrary"),
            vmem_limit_bytes=VMEM_LIMIT_BYTES),
        name="mm",
    )(x, w)


def _sortable_key(score):
    score = jnp.where(score == 0.0, 0.0, score)
    bits = pltpu.bitcast(score, jnp.int32)
    return jnp.where(bits < 0, bits ^ jnp.int32(0x7FFFFFFF), bits)


def _attn_kernel(q_ref, qi_ref, wi_ref, k_ref, v_ref, ki_ref, o_ref,
                 key_ref, thr_ref, jthr_ref, m_ref, l_ref, acc_ref, *, t, topk, scale):
    qb = pl.program_id(1)
    n_kt = qb + 1
    nh = q_ref.shape[1]
    row = lax.broadcasted_iota(jnp.int32, (t, t), 0)
    col = lax.broadcasted_iota(jnp.int32, (t, t), 1)

    qi = qi_ref[0].reshape(nh * t, qi_ref.shape[3])
    wi = wi_ref[0]

    def score_tile(kt, carry):
        ki_t = ki_ref[0, pl.ds(kt * t, t), :]
        rel = lax.dot_general(qi, ki_t, (((1,), (1,)), ((), ())),
                              preferred_element_type=jnp.float32)
        rel = jnp.maximum(rel, 0.0).reshape(nh, t, t)
        score = rel[0] * wi[:, 0:1]
        for h in range(1, nh):
            score = score + rel[h] * wi[:, h:h + 1]
        score = jnp.where(jnp.logical_and(kt == qb, col > row), -jnp.inf, score)
        key_ref[kt] = _sortable_key(score)
        return carry

    lax.fori_loop(0, n_kt, score_tile, 0)

    def count(pred_fn):
        def body(kt, acc):
            c = jnp.where(pred_fn(key_ref[kt], kt), 1, 0).astype(jnp.int32)
            part = c[:, 0:LANES]
            for j in range(1, t // LANES):
                part = part + c[:, j * LANES:(j + 1) * LANES]
            return acc + part
        acc = lax.fori_loop(0, n_kt, body, jnp.zeros((t, LANES), jnp.int32))
        return jnp.sum(acc, axis=1, keepdims=True)

    def count_ge(cand):
        return count(lambda kk, kt: kk >= cand)

    thr0 = jnp.where(count_ge(jnp.zeros((t, 1), jnp.int32)) >= topk,
                     jnp.int32(0), jnp.int32(INT_MIN))

    def bit_step(i, thr):
        cand = thr | jnp.left_shift(jnp.int32(1), 30 - i)
        return jnp.where(count_ge(cand) >= topk, cand, thr)

    thr = lax.fori_loop(0, 31, bit_step, thr0)
    thr_ref[...] = thr
    jthr_ref[...] = jnp.full((t, 1), 2 ** 30, jnp.int32)
    n_ge = count_ge(thr)

    @pl.when(jnp.max(n_ge) > topk)
    def _():
        thr_t = thr_ref[...]
        need = topk - count(lambda kk, kt: kk > thr_t)

        idx_bits = int(np.ceil(np.log2(key_ref.shape[0] * t)))

        def idx_step(i, j):
            cand = j | jnp.left_shift(jnp.int32(1), idx_bits - 1 - i)
            below = count(lambda kk, kt: jnp.logical_and(kk == thr_t, col + kt * t < cand))
            return jnp.where(below < need, cand, j)

        jthr_ref[...] = lax.fori_loop(0, idx_bits, idx_step, jnp.zeros((t, 1), jnp.int32))

    q = q_ref[0].reshape(nh * t, q_ref.shape[3])
    m_ref[...] = jnp.full(m_ref.shape, NEG_BIG, jnp.float32)
    l_ref[...] = jnp.zeros(l_ref.shape, jnp.float32)
    acc_ref[...] = jnp.zeros(acc_ref.shape, jnp.float32)
    thr = thr_ref[...]
    jthr = jthr_ref[...]

    def attn_tile(kt, carry):
        kk = key_ref[kt]
        gcol = col + kt * t
        sel = jnp.logical_or(kk > thr, jnp.logical_and(kk == thr, gcol <= jthr))
        sel = jnp.logical_and(sel, gcol <= row + qb * t)
        bias = jnp.where(sel, 0.0, NEG_BIG)
        k_t = k_ref[0, pl.ds(kt * t, t), :]
        v_t = v_ref[0, pl.ds(kt * t, t), :]
        s = lax.dot_general(q, k_t, (((1,), (1,)), ((), ())),
                            preferred_element_type=jnp.float32) * scale
        s = (s.reshape(nh, t, t) + bias[None]).reshape(nh * t, t)
        m_old = m_ref[...]
        m_new = jnp.maximum(m_old, jnp.max(s, axis=1, keepdims=True))
        alpha = jnp.exp(m_old - m_new)
        p = jnp.exp(s - m_new)
        l_ref[...] = l_ref[...] * alpha + jnp.sum(p, axis=1, keepdims=True)
        acc_ref[...] = acc_ref[...] * alpha + jnp.dot(
            p.astype(jnp.bfloat16), v_t, preferred_element_type=jnp.float32)
        m_ref[...] = m_new
        return carry

    lax.fori_loop(0, n_kt, attn_tile, 0)
    out = acc_ref[...] / l_ref[...]
    dh = q_ref.shape[3]
    for h in range(nh):
        o_ref[0, :, h * dh:(h + 1) * dh] = out[h * t:(h + 1) * t].astype(o_ref.dtype)


def _attention(q, qi, wi, k, v, ki, topk):
    b, nh, s, dh = q.shape
    di = qi.shape[3]
    t = min(ATTN_TILE, s)
    kern = functools.partial(_attn_kernel, t=t, topk=topk, scale=float(dh) ** -0.5)
    return pl.pallas_call(
        kern,
        grid=(b, s // t),
        in_specs=[
            pl.BlockSpec((1, nh, t, dh), lambda bi, qb: (bi, 0, qb, 0)),
            pl.BlockSpec((1, nh, t, di), lambda bi, qb: (bi, 0, qb, 0)),
            pl.BlockSpec((1, t, nh), lambda bi, qb: (bi, qb, 0)),
            pl.BlockSpec((1, s, dh), lambda bi, qb: (bi, 0, 0)),
            pl.BlockSpec((1, s, dh), lambda bi, qb: (bi, 0, 0)),
            pl.BlockSpec((1, s, di), lambda bi, qb: (bi, 0, 0)),
        ],
        out_specs=pl.BlockSpec((1, t, nh * dh), lambda bi, qb: (bi, qb, 0)),
        out_shape=jax.ShapeDtypeStruct((b, s, nh * dh), jnp.bfloat16),
        scratch_shapes=[
            pltpu.VMEM((s // t, t, t), jnp.int32),
            pltpu.VMEM((t, 1), jnp.int32),
            pltpu.VMEM((t, 1), jnp.int32),
            pltpu.VMEM((nh * t, 1), jnp.float32),
            pltpu.VMEM((nh * t, 1), jnp.float32),
            pltpu.VMEM((nh * t, dh), jnp.float32),
        ],
        compiler_params=pltpu.CompilerParams(
            dimension_semantics=("parallel", "arbitrary"),
            vmem_limit_bytes=VMEM_LIMIT_BYTES),
        name="dsa_attention",
    )(q, qi, wi, k, v, ki)


def _bdot(a, b, dims=(((1,), (0,)), ((), ()))):
    return lax.dot_general(a.astype(jnp.bfloat16), b.astype(jnp.bfloat16), dims,
                           preferred_element_type=jnp.float32)


_NT = (((1,), (1,)), ((), ()))
_TN = (((0,), (0,)), ((), ()))


def _scan_kernel(r_ref, lw_ref, k_ref, v_ref, kk_ref, b_ref, y_ref, s_ref, *, chunk):
    @pl.when(pl.program_id(1) == 0)
    def _():
        s_ref[...] = jnp.zeros(s_ref.shape, jnp.float32)

    n_tiles = s_ref.shape[0]
    hl = LANES // 2
    rows = 4 * chunk
    row = lax.broadcasted_iota(jnp.int32, (rows, rows), 0)
    col = lax.broadcasted_iota(jnp.int32, (rows, rows), 1)
    same = (row // chunk) == (col // chunk)
    incl = jnp.logical_and(same, row >= col)
    strict = jnp.logical_and(same, row > col)
    r1 = lax.broadcasted_iota(jnp.int32, (chunk, chunk), 0)
    c1 = lax.broadcasted_iota(jnp.int32, (chunk, chunk), 1)
    tril = jnp.where(r1 >= c1, 1.0, 0.0).astype(jnp.float32)
    lane = lax.broadcasted_iota(jnp.int32, (1, LANES), 1)
    even = lane < hl
    pr_ = lax.broadcasted_iota(jnp.int32, (LANES, LANES), 0)
    pc_ = lax.broadcasted_iota(jnp.int32, (LANES, LANES), 1)
    pair_bd = (pr_ // hl) == (pc_ // hl)

    lw = lw_ref[0]
    c = jnp.dot(tril, lw, precision=lax.Precision.HIGHEST,
                preferred_element_type=jnp.float32)
    e_neg = jnp.exp(-c)
    at = -kk_ref[0] * jnp.exp(c - lw)
    rt = r_ref[0] * jnp.exp(c)
    bt = b_ref[0] * e_neg
    kt = k_ref[0] * e_neg
    v = v_ref[0]
    p_last = jnp.exp(c[chunk - 1:chunk, :])
    bp = bt * p_last
    kp = kt * p_last

    def tile(x, t):
        return x[:, t * LANES:(t + 1) * LANES]

    def stack_masked(x, t0):
        parts = []
        for t in (t0, t0 + 1):
            xt = tile(x, t)
            parts += [jnp.where(even, xt, 0.0), jnp.where(even, 0.0, xt)]
        return jnp.concatenate(parts, axis=0)

    def stack_plain(x, t0):
        return jnp.concatenate([tile(x, t0), tile(x, t0), tile(x, t0 + 1), tile(x, t0 + 1)], axis=0)

    steps = max(1, int(np.ceil(np.log2(chunk))))
    for t0 in range(0, n_tiles, 2):
        at_s = stack_masked(at, t0)
        v_s = stack_masked(v, t0)
        a_all = _bdot(jnp.concatenate([at_s, stack_masked(rt, t0)], axis=0),
                      jnp.concatenate([stack_plain(bt, t0), stack_plain(kt, t0)], axis=0), _NT)
        nmat = jnp.where(strict, a_all[:rows, :rows], 0.0)
        a_ak = jnp.where(strict, a_all[:rows, rows:], 0.0)
        a_rb = jnp.where(incl, a_all[rows:, :rows], 0.0)
        a_rk = jnp.where(incl, a_all[rows:, rows:], 0.0)

        x = jnp.concatenate([at_s, _bdot(a_ak, v_s)], axis=1)
        mpow = nmat
        for i in range(steps):
            x = x + _bdot(mpow, x)
            if i + 1 < steps:
                mpow = _bdot(mpow, mpow)
        qy = _bdot(a_rb, x)
        y0_s = qy[:, LANES:] + _bdot(a_rk, v_s)
        qh_s = qy[:, :LANES]

        for j in range(2):
            t = t0 + j
            lo, mid, hi = 2 * j * chunk, (2 * j + 1) * chunk, (2 * j + 2) * chunk
            unstack = lambda z: z[lo:mid] + z[mid:hi]
            x_t = unstack(x)
            qh_t = tile(rt, t) + unstack(qh_s)
            y0_t = unstack(y0_s)
            s0 = s_ref[t]
            gw = _bdot(x_t, tile(bp, t), _TN)
            g_t = jnp.where(pair_bd, gw[:LANES], 0.0)
            h_t = jnp.where(pair_bd, gw[LANES:] + _bdot(tile(v, t), tile(kp, t), _TN), 0.0)
            y_ref[0, :, t * LANES:(t + 1) * LANES] = y0_t + _bdot(qh_t, s0, _NT)
            s_ref[t] = s0 * tile(p_last, t) + _bdot(s0, g_t) + h_t


def _rwkv_scan(r, lw, k, v, kk, b):
    bsz, s, d = r.shape
    chunk = min(SCAN_CHUNK, s)
    spec = pl.BlockSpec((1, chunk, d), lambda bi, ci: (bi, ci, 0))
    kern = functools.partial(_scan_kernel, chunk=chunk)
    return pl.pallas_call(
        kern,
        grid=(bsz, s // chunk),
        in_specs=[spec] * 6,
        out_specs=spec,
        out_shape=jax.ShapeDtypeStruct((bsz, s, d), jnp.float32),
        scratch_shapes=[pltpu.VMEM((d // LANES, LANES, LANES), jnp.float32)],
        compiler_params=pltpu.CompilerParams(
            dimension_semantics=("parallel", "arbitrary"),
            vmem_limit_bytes=VMEM_LIMIT_BYTES),
        name="rwkv7_scan",
    )(r, lw, k, v, kk, b)


def _layer_norm(x, g, b, eps=LN_EPS):
    mu = jnp.mean(x, -1, keepdims=True)
    var = jnp.mean(jnp.square(x - mu), -1, keepdims=True)
    return (x - mu) * lax.rsqrt(var + eps) * g + b


def _rms_norm(x, g, eps=1e-6):
    return x * lax.rsqrt(jnp.mean(jnp.square(x), -1, keepdims=True) + eps) * g


def _rope_tables(positions, rot_dim):
    inv_freq = jnp.power(jnp.float32(ROPE_THETA), -jnp.arange(0, rot_dim, 2, dtype=jnp.float32) / rot_dim)
    ang = positions.astype(jnp.float32)[..., None] * inv_freq
    return jnp.cos(ang)[:, :, None, :], jnp.sin(ang)[:, :, None, :]


def _partial_rope(x, cos, sin):
    half = cos.shape[-1]
    x1, x2, rest = x[..., :half], x[..., half:2 * half], x[..., 2 * half:]
    return jnp.concatenate([x1 * cos - x2 * sin, x2 * cos + x1 * sin, rest], -1)


def _token_shift(p, mu):
    prev = jnp.pad(p, ((0, 0), (1, 0), (0, 0)))[:, :-1]
    return p + (prev - p) * mu


def _pad_w_in(w, has_vr):
    k = w.shape[0]
    z = lambda n: jnp.zeros((k, n), w.dtype)
    parts = [w[:, :N_ATT], z(N_ATT_PAD - N_ATT), w[:, N_ATT:N_IN0]]
    used = N_ATT_PAD + N_GATE + N_RWKV
    if has_vr:
        parts.append(w[:, N_IN0:])
        used += LORA_V
    parts.append(z(N_IN_PAD - used))
    return jnp.concatenate(parts, axis=1).astype(jnp.bfloat16)


def kernel(x, positions, w_in0, w_in_rest, b_gate, g_cq, w_uq, w_iq, g_ik, b_ik, w_oa, mu_rwkv, mu_vres, w0, w2, a0, a2, v0, v2, g2, k_k, k_a, r_k, g_lnx, b_lnx, w_ob, w_out, ln1_g, ln1_b, w_up, conv_w, conv_b, w_down, ln2_g, ln2_b):
    bsz, seq, d = x.shape
    m = bsz * seq
    alpha = (2 * DEPTH) ** 0.25
    topk = min(TOPK_MAX, seq // 4)
    rope_a = _rope_tables(positions, ROT_A)
    rope_i = _rope_tables(positions, ROT_IDX)
    v_first = None
    mm3 = lambda t, w: _mm(t.reshape(m, t.shape[-1]), w).reshape(bsz, seq, w.shape[1])

    for i in range(DEPTH):
        w_in = _pad_w_in(w_in0 if i == 0 else w_in_rest[i - 1], i > 0)
        p = mm3(x, w_in)

        cq_raw = p[..., :D_CQ]
        k_raw = p[..., D_CQ:D_CQ + DH_A]
        v_raw = p[..., D_CQ + DH_A:D_CQ + 2 * DH_A]
        ik_raw = p[..., D_CQ + 2 * DH_A:D_CQ + 2 * DH_A + D_IDX]
        iw_raw = p[..., D_CQ + 2 * DH_A + D_IDX:N_ATT]
        c_q = _rms_norm(cq_raw, g_cq[i])
        q = _partial_rope(mm3(c_q, w_uq[i]).reshape(bsz, seq, H_A, DH_A), *rope_a)
        k_att = _partial_rope(k_raw[:, :, None, :], *rope_a)[:, :, 0]
        qi = _partial_rope(mm3(c_q, w_iq[i]).reshape(bsz, seq, H_IDX, D_IDX), *rope_i)
        ki = _partial_rope(_layer_norm(ik_raw, g_ik[i], b_ik[i])[:, :, None, :], *rope_i)[:, :, 0]
        wi = iw_raw * (H_IDX ** -0.5 * D_IDX ** -0.5)
        o = _attention(q.swapaxes(1, 2).astype(jnp.bfloat16), qi.swapaxes(1, 2).astype(jnp.bfloat16), wi,
                       k_att.astype(jnp.bfloat16), v_raw.astype(jnp.bfloat16), ki.astype(jnp.bfloat16), topk)
        y_a = mm3(o, w_oa[i])

        p_rw = _token_shift(p[..., OFF_RW:OFF_RW + N_RWKV], mu_rwkv[i])
        pr = p_rw[..., :D_R]
        pk = p_rw[..., D_R:2 * D_R]
        pv = p_rw[..., 2 * D_R:3 * D_R]
        pw = p_rw[..., 3 * D_R:3 * D_R + LORA_W]
        pa = p_rw[..., 3 * D_R + LORA_W:3 * D_R + LORA_W + LORA_A]
        pg = p_rw[..., 3 * D_R + LORA_W + LORA_A:]
        if i == 0:
            v_first = pv
            v_r = pv
        else:
            p_vr = _token_shift(p[..., OFF_VR:OFF_VR + LORA_V], mu_vres[i - 1])
            v_r = pv + (v_first - pv) * jax.nn.sigmoid(v0[i - 1] + mm3(p_vr, v2[i - 1]))
        w_dec = -jax.nn.softplus(-(w0[i] + mm3(jnp.tanh(pw), w2[i]))) - 0.5
        lw = -jnp.exp(w_dec)
        a_r = jax.nn.sigmoid(a0[i] + mm3(pa, a2[i]))
        g_r = mm3(jax.nn.sigmoid(pg), g2[i])
        hd = lambda t_: t_.reshape(bsz, seq, H_R, N_R)
        kk = hd(pk * k_k[i])
        kk = kk / jnp.maximum(jnp.sqrt(jnp.sum(kk * kk, -1, keepdims=True)), 1e-12)
        k_r = pk * (1 + (a_r - 1) * k_a[i])
        r_h, k_h, v_h = hd(pr), hd(k_r), hd(v_r)
        kk_f = kk.reshape(bsz, seq, D_R)
        y = hd(_rwkv_scan(pr, lw, k_r, v_r, kk_f, kk_f * a_r))
        mu_y = jnp.mean(y, -1, keepdims=True)
        var_y = jnp.mean(jnp.square(y - mu_y), -1, keepdims=True)
        y = ((y - mu_y) * lax.rsqrt(var_y + GN_EPS)).reshape(bsz, seq, D_R) * g_lnx[i] + b_lnx[i]
        bonus = jnp.sum(r_h * k_h * r_k[i], -1, keepdims=True) * v_h
        y = y + bonus.reshape(bsz, seq, D_R)
        y_b = mm3(y * g_r, w_ob[i])

        gates = jax.nn.sigmoid(p[..., OFF_GATE:OFF_GATE + N_GATE] + b_gate[i])
        z = gates[..., :D_MODEL] * y_a + gates[..., D_MODEL:] * y_b
        x = _layer_norm(alpha * x + mm3(z, w_out[i]), ln1_g[i], ln1_b[i])

        hcat = mm3(x, w_up[i])
        h_gate, h_val = hcat[..., :D_FF], hcat[..., D_FF:]
        gp = jnp.pad(h_gate, ((0, 0), (2, 0), (0, 0)))
        conv = gp[:, :-2] * conv_w[i, 0] + gp[:, 1:-1] * conv_w[i, 1] + gp[:, 2:] * conv_w[i, 2] + conv_b[i]
        act = jax.nn.silu(conv) * h_val
        x = _layer_norm(alpha * x + mm3(act, w_down[i]), ln2_g[i], ln2_b[i])
    return x
```

```python
import functools

import jax
import jax.numpy as jnp
import numpy as np
from jax import lax
from jax.experimental import pallas as pl
from jax.experimental.pallas import tpu as pltpu

D_MODEL = 1024
DEPTH = 4
H_A = 8
DH_A = 128
D_CQ = 256
ROT_A = DH_A // 4
H_IDX = 8
D_IDX = 64
ROT_IDX = D_IDX // 4
TOPK_MAX = 256
ROPE_THETA = 500000.0
N_R = 64
H_R = D_MODEL // N_R
D_R = H_R * N_R
LORA_W = 64
LORA_A = 64
LORA_V = 32
LORA_G = 128
GN_EPS = 64e-5
D_FF = ((8 * D_MODEL // 3 + 127) // 128) * 128
LN_EPS = 1e-5

N_ATT = D_CQ + 2 * DH_A + D_IDX + H_IDX
N_GATE = 2 * D_MODEL
N_RWKV = 3 * D_R + LORA_W + LORA_A + LORA_G
N_IN0 = N_ATT + N_GATE + N_RWKV

LANES = 128
VMEM_LIMIT_BYTES = 48 * 1024 * 1024

N_ATT_PAD = 640
N_RW_PAD = 3584

ATTN_TILE = 256
SCAN_CHUNK = 64
NEG_BIG = -1e30
INT_MIN = -(2 ** 31)


def _mm_kernel(x_ref, w_ref, o_ref):
    o_ref[...] = jnp.dot(x_ref[...].astype(jnp.bfloat16), w_ref[...],
                         preferred_element_type=jnp.float32).astype(o_ref.dtype)


def _pick_tile(n, cap):
    best = None
    for t in range(LANES, min(n, cap) + 1, LANES):
        if n % t == 0:
            best = t
    return best if best is not None else n


def _mm(x, w, out_dtype=jnp.float32, tm=None):
    m, k = x.shape
    n = w.shape[1]
    w = w.astype(jnp.bfloat16)
    if tm is None:
        tm = 1024 if k <= 1024 else 512
    tm = min(tm, m)
    tn = _pick_tile(n, 1024 if k <= 1024 else 512)
    return pl.pallas_call(
        _mm_kernel,
        grid=(m // tm, n // tn),
        in_specs=[pl.BlockSpec((tm, k), lambda i, j: (i, 0)),
                  pl.BlockSpec((k, tn), lambda i, j: (0, j))],
        out_specs=pl.BlockSpec((tm, tn), lambda i, j: (i, j)),
        out_shape=jax.ShapeDtypeStruct((m, n), out_dtype),
        compiler_params=pltpu.CompilerParams(
            dimension_semantics=("parallel", "arbitrary"),
            vmem_limit_bytes=VMEM_LIMIT_BYTES),
        name="mm",
    )(x, w)


def _gate_kernel(x_ref, w_ref, b_ref, o_ref):
    p = jnp.dot(x_ref[...].astype(jnp.bfloat16), w_ref[...], preferred_element_type=jnp.float32)
    o_ref[...] = jax.nn.sigmoid(p + b_ref[...])


def _mm_gate(x, w, bias):
    m, k = x.shape
    n = w.shape[1]
    tm = min(1024, m)
    tn = _pick_tile(n, 1024)
    return pl.pallas_call(
        _gate_kernel,
        grid=(m // tm, n // tn),
        in_specs=[pl.BlockSpec((tm, k), lambda i, j: (i, 0)),
                  pl.BlockSpec((k, tn), lambda i, j: (0, j)),
                  pl.BlockSpec((1, tn), lambda i, j: (0, j))],
        out_specs=pl.BlockSpec((tm, tn), lambda i, j: (i, j)),
        out_shape=jax.ShapeDtypeStruct((m, n), jnp.float32),
        compiler_params=pltpu.CompilerParams(
            dimension_semantics=("parallel", "arbitrary"),
            vmem_limit_bytes=VMEM_LIMIT_BYTES),
        name="mm_gate",
    )(x, w.astype(jnp.bfloat16), bias.reshape(1, n))


def _shift_rows(cur, prev_rows, k):
    rolled = pltpu.roll(cur, k, 0)
    row = lax.broadcasted_iota(jnp.int32, cur.shape, 0)
    for i in range(k):
        rolled = jnp.where(row == i, prev_rows[i:i + 1, :], rolled)
    return rolled


def _shift_mm_kernel(x_ref, w_ref, mu_ref, o_ref, carry_ref):
    si = pl.program_id(1)
    j = pl.program_id(2)
    tm = o_ref.shape[1]
    p = jnp.dot(x_ref[0].astype(jnp.bfloat16), w_ref[...], preferred_element_type=jnp.float32)
    @pl.when(si == 0)
    def _():
        carry_ref[j] = jnp.zeros(carry_ref.shape[1:], jnp.float32)

    prev = _shift_rows(p, carry_ref[j, 0:1, :], 1)
    carry_ref[j, 0:1, :] = p[tm - 1:tm, :]
    o_ref[0] = p + (prev - p) * mu_ref[...]


def _mm_shift(x, w, mu):
    b, s, k = x.shape
    n = w.shape[1]
    tm = min(1024, s)
    tn = 512 if n % 512 == 0 else _pick_tile(n, 1024)
    return pl.pallas_call(
        _shift_mm_kernel,
        grid=(b, s // tm, n // tn),
        in_specs=[pl.BlockSpec((1, tm, k), lambda bi, si, j: (bi, si, 0)),
                  pl.BlockSpec((k, tn), lambda bi, si, j: (0, j)),
                  pl.BlockSpec((1, tn), lambda bi, si, j: (0, j))],
        out_specs=pl.BlockSpec((1, tm, tn), lambda bi, si, j: (bi, si, j)),
        out_shape=jax.ShapeDtypeStruct((b, s, n), jnp.float32),
        scratch_shapes=[pltpu.VMEM((n // tn, 8, tn), jnp.float32)],
        compiler_params=pltpu.CompilerParams(
            dimension_semantics=("parallel", "arbitrary", "arbitrary"),
            vmem_limit_bytes=VMEM_LIMIT_BYTES),
        name="mm_shift",
    )(x, w.astype(jnp.bfloat16), mu.reshape(1, n))


def _ffn_kernel(x_ref, wg_ref, wv_ref, cw_ref, cb_ref, wd_ref, g_ref, b_ref, o_ref,
                xb_ref, acc_ref, carry_ref, *, alpha, eps):
    si = pl.program_id(1)
    c = pl.program_id(2)
    tm = o_ref.shape[1]

    @pl.when(c == 0)
    def _():
        xb_ref[...] = x_ref[0].astype(jnp.bfloat16)
        acc_ref[...] = jnp.zeros(acc_ref.shape, jnp.float32)

    xb = xb_ref[...]
    hg = jnp.dot(xb, wg_ref[...], preferred_element_type=jnp.float32)
    hv = jnp.dot(xb, wv_ref[...], preferred_element_type=jnp.float32)
    @pl.when(si == 0)
    def _():
        carry_ref[c] = jnp.zeros(carry_ref.shape[1:], jnp.float32)

    tail = carry_ref[c, 0:2, :]
    g1 = _shift_rows(hg, tail[1:2, :], 1)
    g2 = _shift_rows(hg, tail, 2)
    carry_ref[c, 0:2, :] = hg[tm - 2:tm, :]
    cw = cw_ref[...]
    conv = g2 * cw[0:1, :] + g1 * cw[1:2, :] + hg * cw[2:3, :] + cb_ref[...]
    act = jax.nn.silu(conv) * hv
    acc_ref[...] += jnp.dot(act.astype(jnp.bfloat16), wd_ref[...], preferred_element_type=jnp.float32)

    @pl.when(c == pl.num_programs(2) - 1)
    def _():
        y = alpha * x_ref[0] + acc_ref[...]
        mu = jnp.mean(y, axis=-1, keepdims=True)
        d = y - mu
        var = jnp.mean(d * d, axis=-1, keepdims=True)
        o_ref[0] = d * lax.rsqrt(var + eps) * g_ref[...] + b_ref[...]


def _ffn(x, w_up, conv_w, conv_b, w_down, ln_g, ln_b, alpha):
    b, s, d = x.shape
    dff = w_down.shape[0]
    tm = min(1024, s)
    tf = 256 if dff % 256 == 0 else LANES
    nc = dff // tf
    kern = functools.partial(_ffn_kernel, alpha=alpha, eps=LN_EPS)
    w_up = w_up.astype(jnp.bfloat16)
    return pl.pallas_call(
        kern,
        grid=(b, s // tm, nc),
        in_specs=[
            pl.BlockSpec((1, tm, d), lambda bi, si, c: (bi, si, 0)),
            pl.BlockSpec((d, tf), lambda bi, si, c: (0, c)),
            pl.BlockSpec((d, tf), lambda bi, si, c: (0, c + nc)),
            pl.BlockSpec((3, tf), lambda bi, si, c: (0, c)),
            pl.BlockSpec((1, tf), lambda bi, si, c: (0, c)),
            pl.BlockSpec((tf, d), lambda bi, si, c: (c, 0)),
            pl.BlockSpec((1, d), lambda bi, si, c: (0, 0)),
            pl.BlockSpec((1, d), lambda bi, si, c: (0, 0)),
        ],
        out_specs=pl.BlockSpec((1, tm, d), lambda bi, si, c: (bi, si, 0)),
        out_shape=jax.ShapeDtypeStruct((b, s, d), jnp.float32),
        scratch_shapes=[
            pltpu.VMEM((tm, d), jnp.bfloat16),
            pltpu.VMEM((tm, d), jnp.float32),
            pltpu.VMEM((nc, 8, tf), jnp.float32),
        ],
        compiler_params=pltpu.CompilerParams(
            dimension_semantics=("parallel", "arbitrary", "arbitrary"),
            vmem_limit_bytes=VMEM_LIMIT_BYTES),
        name="ffn",
    )(x, w_up, w_up, conv_w, conv_b.reshape(1, dff), w_down.astype(jnp.bfloat16),
      ln_g.reshape(1, d), ln_b.reshape(1, d))


def _sortable_key(score):
    score = jnp.where(score == 0.0, 0.0, score)
    bits = pltpu.bitcast(score, jnp.int32)
    return jnp.where(bits < 0, bits ^ jnp.int32(0x7FFFFFFF), bits)


def _attn_kernel(qt_ref, qit_ref, wi_ref, k_ref, vt_ref, ki_ref, o_ref,
                 key_ref, jthr_ref, m_ref, l_ref, acc_ref, *, t, nh, topk, scale):
    qb = pl.program_id(1)
    n_kt = qb + 1
    krow = lax.broadcasted_iota(jnp.int32, (t, t), 0)
    qcol = lax.broadcasted_iota(jnp.int32, (t, t), 1)

    qit = qit_ref[0, 0]
    wi = wi_ref[0, 0]

    def score_tile(kt, carry):
        ki_t = ki_ref[0, pl.ds(kt * t, t), :]
        rel = jnp.dot(ki_t, qit, preferred_element_type=jnp.float32)
        rel = jnp.maximum(rel, 0.0) * wi
        score = rel[:, 0:t]
        for h in range(1, nh):
            score = score + rel[:, h * t:(h + 1) * t]
        score = jnp.where(jnp.logical_and(kt == qb, krow > qcol), -jnp.inf, score)
        key_ref[kt] = _sortable_key(score)
        return carry

    lax.fori_loop(0, n_kt, score_tile, 0)

    def count(pred_fn):
        def body(kt, acc):
            c = jnp.where(pred_fn(key_ref[kt], kt), 1, 0).astype(jnp.int32)
            return acc + jnp.sum(c.reshape(t // 8, 8, t), axis=0)
        acc = lax.fori_loop(0, n_kt, body, jnp.zeros((8, t), jnp.int32))
        return jnp.sum(acc, axis=0, keepdims=True)

    def count_ge(cand):
        return count(lambda kk, kt: kk >= cand)

    thr0 = jnp.where(count_ge(jnp.zeros((1, t), jnp.int32)) >= topk,
                     jnp.int32(0), jnp.int32(INT_MIN))

    def bit_step(i, thr):
        cand = thr | jnp.left_shift(jnp.int32(1), 30 - i)
        return jnp.where(count_ge(cand) >= topk, cand, thr)

    thr = lax.fori_loop(0, 31, bit_step, thr0)
    jthr_ref[...] = jnp.full(jthr_ref.shape, 2 ** 30, jnp.int32)
    n_ge = count_ge(thr)

    @pl.when(jnp.max(n_ge) > topk)
    def _():
        need = topk - count(lambda kk, kt: kk > thr)
        idx_bits = int(np.ceil(np.log2(key_ref.shape[0] * t)))

        def idx_step(i, j):
            cand = j | jnp.left_shift(jnp.int32(1), idx_bits - 1 - i)
            below = count(lambda kk, kt: jnp.logical_and(kk == thr, krow + kt * t < cand))
            return jnp.where(below < need, cand, j)

        jthr_ref[...] = lax.fori_loop(0, idx_bits, idx_step, jnp.zeros((1, t), jnp.int32))

    qt = qt_ref[0, 0]
    m_ref[...] = jnp.full(m_ref.shape, NEG_BIG, jnp.float32)
    l_ref[...] = jnp.zeros(l_ref.shape, jnp.float32)
    acc_ref[...] = jnp.zeros(acc_ref.shape, jnp.float32)
    jthr = jthr_ref[...]

    def attn_tile(kt, carry):
        kk = key_ref[kt]
        gk = krow + kt * t
        sel = jnp.logical_or(kk > thr, jnp.logical_and(kk == thr, gk <= jthr))
        sel = jnp.logical_and(sel, gk <= qcol + qb * t)
        bias = jnp.where(sel, 0.0, NEG_BIG)
        k_t = k_ref[0, pl.ds(kt * t, t), :]
        vt_t = vt_ref[0, kt]
        s_all = jnp.dot(k_t, qt, preferred_element_type=jnp.float32)
        for h in range(nh):
            s = s_all[:, h * t:(h + 1) * t] * scale + bias
            m_old = m_ref[h:h + 1, :]
            m_new = jnp.maximum(m_old, jnp.max(s, axis=0, keepdims=True))
            alpha = jnp.exp(m_old - m_new)
            p = jnp.exp(s - m_new)
            l_ref[h:h + 1, :] = l_ref[h:h + 1, :] * alpha + jnp.sum(p, axis=0, keepdims=True)
            acc_ref[h] = acc_ref[h] * alpha + jnp.dot(
                vt_t, p.astype(jnp.bfloat16), preferred_element_type=jnp.float32)
            m_ref[h:h + 1, :] = m_new
        return carry

    lax.fori_loop(0, n_kt, attn_tile, 0)
    dh = acc_ref.shape[1]
    for h in range(nh):
        out = acc_ref[h] / l_ref[h:h + 1, :]
        o_ref[0, :, h * dh:(h + 1) * dh] = out.T.astype(o_ref.dtype)


def _attention(q, qi, wi, k, v, ki, topk):
    b, s, nh, dh = q.shape
    di = qi.shape[3]
    t = min(ATTN_TILE, s)
    nq = s // t
    bf = jnp.bfloat16
    qt = q.astype(bf).reshape(b, nq, t, nh, dh).transpose(0, 1, 4, 3, 2).reshape(b, nq, dh, nh * t)
    qit = qi.astype(bf).reshape(b, nq, t, nh, di).transpose(0, 1, 4, 3, 2).reshape(b, nq, di, nh * t)
    wit = wi.reshape(b, nq, t, nh).transpose(0, 1, 3, 2).reshape(b, nq, 1, nh * t)
    vt = v.astype(bf).reshape(b, nq, t, dh).swapaxes(2, 3)
    kern = functools.partial(_attn_kernel, t=t, nh=nh, topk=topk, scale=float(dh) ** -0.5)
    return pl.pallas_call(
        kern,
        grid=(b, nq),
        in_specs=[
            pl.BlockSpec((1, 1, dh, nh * t), lambda bi, qb: (bi, qb, 0, 0)),
            pl.BlockSpec((1, 1, di, nh * t), lambda bi, qb: (bi, qb, 0, 0)),
            pl.BlockSpec((1, 1, 1, nh * t), lambda bi, qb: (bi, qb, 0, 0)),
            pl.BlockSpec((1, s, dh), lambda bi, qb: (bi, 0, 0)),
            pl.BlockSpec((1, nq, dh, t), lambda bi, qb: (bi, 0, 0, 0)),
            pl.BlockSpec((1, s, di), lambda bi, qb: (bi, 0, 0)),
        ],
        out_specs=pl.BlockSpec((1, t, nh * dh), lambda bi, qb: (bi, qb, 0)),
        out_shape=jax.ShapeDtypeStruct((b, s, nh * dh), jnp.bfloat16),
        scratch_shapes=[
            pltpu.VMEM((nq, t, t), jnp.int32),
            pltpu.VMEM((1, t), jnp.int32),
            pltpu.VMEM((nh, t), jnp.float32),
            pltpu.VMEM((nh, t), jnp.float32),
            pltpu.VMEM((nh, dh, t), jnp.float32),
        ],
        compiler_params=pltpu.CompilerParams(
            dimension_semantics=("parallel", "arbitrary"),
            vmem_limit_bytes=VMEM_LIMIT_BYTES),
        name="dsa_attention",
    )(qt, qit, wit, k.astype(bf), vt, ki.astype(bf))


def _bdot(a, b, dims=(((1,), (0,)), ((), ()))):
    return lax.dot_general(a.astype(jnp.bfloat16), b.astype(jnp.bfloat16), dims,
                           preferred_element_type=jnp.float32)


_NT = (((1,), (1,)), ((), ()))
_TN = (((0,), (0,)), ((), ()))


def _scan_kernel(r_ref, lw_ref, k_ref, v_ref, kk_ref, b_ref, y_ref, s_ref, *, chunk):
    @pl.when(pl.program_id(1) == 0)
    def _():
        s_ref[...] = jnp.zeros(s_ref.shape, jnp.float32)

    n_tiles = s_ref.shape[0]
    hl = LANES // 2
    rows = 4 * chunk
    row = lax.broadcasted_iota(jnp.int32, (rows, rows), 0)
    col = lax.broadcasted_iota(jnp.int32, (rows, rows), 1)
    same = (row // chunk) == (col // chunk)
    incl = jnp.logical_and(same, row >= col)
    strict = jnp.logical_and(same, row > col)
    r1 = lax.broadcasted_iota(jnp.int32, (chunk, chunk), 0)
    c1 = lax.broadcasted_iota(jnp.int32, (chunk, chunk), 1)
    tril = jnp.where(r1 >= c1, 1.0, 0.0).astype(jnp.float32)
    lane = lax.broadcasted_iota(jnp.int32, (1, LANES), 1)
    even = lane < hl
    pr_ = lax.broadcasted_iota(jnp.int32, (LANES, LANES), 0)
    pc_ = lax.broadcasted_iota(jnp.int32, (LANES, LANES), 1)
    pair_bd = (pr_ // hl) == (pc_ // hl)

    lw = lw_ref[0]
    c = jnp.dot(tril, lw, precision=lax.Precision.HIGHEST,
                preferred_element_type=jnp.float32)
    e_neg = jnp.exp(-c)
    at = -kk_ref[0] * jnp.exp(c - lw)
    rt = r_ref[0] * jnp.exp(c)
    bt = b_ref[0] * e_neg
    kt = k_ref[0] * e_neg
    v = v_ref[0]
    p_last = jnp.exp(c[chunk - 1:chunk, :])
    bp = bt * p_last
    kp = kt * p_last

    def tile(x, t):
        return x[:, t * LANES:(t + 1) * LANES]

    def stack_masked(x, t0):
        parts = []
        for t in (t0, t0 + 1):
            xt = tile(x, t)
            parts += [jnp.where(even, xt, 0.0), jnp.where(even, 0.0, xt)]
        return jnp.concatenate(parts, axis=0)

    def stack_plain(x, t0):
        return jnp.concatenate([tile(x, t0), tile(x, t0), tile(x, t0 + 1), tile(x, t0 + 1)], axis=0)

    steps = max(1, int(np.ceil(np.log2(chunk))))
    for t0 in range(0, n_tiles, 2):
        at_s = stack_masked(at, t0)
        v_s = stack_masked(v, t0)
        a_all = _bdot(jnp.concatenate([at_s, stack_masked(rt, t0)], axis=0),
                      jnp.concatenate([stack_plain(bt, t0), stack_plain(kt, t0)], axis=0), _NT)
        nmat = jnp.where(strict, a_all[:rows, :rows], 0.0)
        a_ak = jnp.where(strict, a_all[:rows, rows:], 0.0)
        a_rb = jnp.where(incl, a_all[rows:, :rows], 0.0)
        a_rk = jnp.where(incl, a_all[rows:, rows:], 0.0)

        x = jnp.concatenate([at_s, _bdot(a_ak, v_s)], axis=1)
        mpow = nmat
        for i in range(steps):
            x = x + _bdot(mpow, x)
            if i + 1 < steps:
                mpow = _bdot(mpow, mpow)
        qy = _bdot(a_rb, x)
        y0_s = qy[:, LANES:] + _bdot(a_rk, v_s)
        qh_s = qy[:, :LANES]

        for j in range(2):
            t = t0 + j
            lo, mid, hi = 2 * j * chunk, (2 * j + 1) * chunk, (2 * j + 2) * chunk
            unstack = lambda z: z[lo:mid] + z[mid:hi]
            x_t = unstack(x)
            qh_t = tile(rt, t) + unstack(qh_s)
            y0_t = unstack(y0_s)
            s0 = s_ref[t]
            gw = _bdot(x_t, tile(bp, t), _TN)
            g_t = jnp.where(pair_bd, gw[:LANES], 0.0)
            h_t = jnp.where(pair_bd, gw[LANES:] + _bdot(tile(v, t), tile(kp, t), _TN), 0.0)
            y_ref[0, :, t * LANES:(t + 1) * LANES] = y0_t + _bdot(qh_t, s0, _NT)
            s_ref[t] = s0 * tile(p_last, t) + _bdot(s0, g_t) + h_t


def _rwkv_scan(r, lw, k, v, kk, b):
    bsz, s, d = r.shape
    chunk = min(SCAN_CHUNK, s)
    spec = pl.BlockSpec((1, chunk, d), lambda bi, ci: (bi, ci, 0))
    kern = functools.partial(_scan_kernel, chunk=chunk)
    return pl.pallas_call(
        kern,
        grid=(bsz, s // chunk),
        in_specs=[spec] * 6,
        out_specs=spec,
        out_shape=jax.ShapeDtypeStruct((bsz, s, d), jnp.float32),
        scratch_shapes=[pltpu.VMEM((d // LANES, LANES, LANES), jnp.float32)],
        compiler_params=pltpu.CompilerParams(
            dimension_semantics=("parallel", "arbitrary"),
            vmem_limit_bytes=VMEM_LIMIT_BYTES),
        name="rwkv7_scan",
    )(r, lw, k, v, kk, b)


def _layer_norm(x, g, b, eps=LN_EPS):
    mu = jnp.mean(x, -1, keepdims=True)
    var = jnp.mean(jnp.square(x - mu), -1, keepdims=True)
    return (x - mu) * lax.rsqrt(var + eps) * g + b


def _rms_norm(x, g, eps=1e-6):
    return x * lax.rsqrt(jnp.mean(jnp.square(x), -1, keepdims=True) + eps) * g


def _rope_tables(positions, rot_dim):
    inv_freq = jnp.power(jnp.float32(ROPE_THETA), -jnp.arange(0, rot_dim, 2, dtype=jnp.float32) / rot_dim)
    ang = positions.astype(jnp.float32)[..., None] * inv_freq
    return jnp.cos(ang)[:, :, None, :], jnp.sin(ang)[:, :, None, :]


def _partial_rope(x, cos, sin):
    half = cos.shape[-1]
    x1, x2, rest = x[..., :half], x[..., half:2 * half], x[..., 2 * half:]
    return jnp.concatenate([x1 * cos - x2 * sin, x2 * cos + x1 * sin, rest], -1)


def kernel(x, positions, w_in0, w_in_rest, b_gate, g_cq, w_uq, w_iq, g_ik, b_ik, w_oa, mu_rwkv, mu_vres, w0, w2, a0, a2, v0, v2, g2, k_k, k_a, r_k, g_lnx, b_lnx, w_ob, w_out, ln1_g, ln1_b, w_up, conv_w, conv_b, w_down, ln2_g, ln2_b):
    bsz, seq, d = x.shape
    m = bsz * seq
    alpha = (2 * DEPTH) ** 0.25
    topk = min(TOPK_MAX, seq // 4)
    rope_a = _rope_tables(positions, ROT_A)
    rope_i = _rope_tables(positions, ROT_IDX)
    v_first = None
    mm3 = lambda t, w: _mm(t.reshape(m, t.shape[-1]), w).reshape(bsz, seq, w.shape[1])

    for i in range(DEPTH):
        w_in = w_in0 if i == 0 else w_in_rest[i - 1]
        zcols = lambda n: jnp.zeros((d, n), w_in.dtype)
        w_att = jnp.concatenate([w_in[:, :N_ATT], zcols(N_ATT_PAD - N_ATT)], axis=1)
        n_rw = w_in.shape[1] - (N_ATT + N_GATE)
        w_rw = jnp.concatenate([w_in[:, N_ATT + N_GATE:], zcols(N_RW_PAD - n_rw)], axis=1)
        mu_parts = [mu_rwkv[i]] + ([mu_vres[i - 1]] if i > 0 else [])
        mu_rw = jnp.concatenate(mu_parts + [jnp.zeros((N_RW_PAD - n_rw,), jnp.float32)])
        p_att = mm3(x, w_att)
        gates = _mm_gate(x.reshape(m, d), w_in[:, N_ATT:N_ATT + N_GATE], b_gate[i]).reshape(bsz, seq, N_GATE)
        p_rw = _mm_shift(x, w_rw, mu_rw)

        cq_raw = p_att[..., :D_CQ]
        k_raw = p_att[..., D_CQ:D_CQ + DH_A]
        v_raw = p_att[..., D_CQ + DH_A:D_CQ + 2 * DH_A]
        ik_raw = p_att[..., D_CQ + 2 * DH_A:D_CQ + 2 * DH_A + D_IDX]
        iw_raw = p_att[..., D_CQ + 2 * DH_A + D_IDX:N_ATT]
        c_q = _rms_norm(cq_raw, g_cq[i])
        q = _partial_rope(mm3(c_q, w_uq[i]).reshape(bsz, seq, H_A, DH_A), *rope_a)
        k_att = _partial_rope(k_raw[:, :, None, :], *rope_a)[:, :, 0]
        qi = _partial_rope(mm3(c_q, w_iq[i]).reshape(bsz, seq, H_IDX, D_IDX), *rope_i)
        ki = _partial_rope(_layer_norm(ik_raw, g_ik[i], b_ik[i])[:, :, None, :], *rope_i)[:, :, 0]
        wi = iw_raw * (H_IDX ** -0.5 * D_IDX ** -0.5)
        o = _attention(q, qi, wi, k_att, v_raw, ki, topk)
        y_a = mm3(o, w_oa[i])

        pr = p_rw[..., :D_R]
        pk = p_rw[..., D_R:2 * D_R]
        pv = p_rw[..., 2 * D_R:3 * D_R]
        pw = p_rw[..., 3 * D_R:3 * D_R + LORA_W]
        pa = p_rw[..., 3 * D_R + LORA_W:3 * D_R + LORA_W + LORA_A]
        pg = p_rw[..., 3 * D_R + LORA_W + LORA_A:N_RWKV]
        if i == 0:
            v_first = pv
            v_r = pv
        else:
            p_vr = p_rw[..., N_RWKV:N_RWKV + LORA_V]
            v_r = pv + (v_first - pv) * jax.nn.sigmoid(v0[i - 1] + mm3(p_vr, v2[i - 1]))
        w_dec = -jax.nn.softplus(-(w0[i] + mm3(jnp.tanh(pw), w2[i]))) - 0.5
        lw = -jnp.exp(w_dec)
        a_r = jax.nn.sigmoid(a0[i] + mm3(pa, a2[i]))
        g_r = mm3(jax.nn.sigmoid(pg), g2[i])
        hd = lambda t_: t_.reshape(bsz, seq, H_R, N_R)
        kk = hd(pk * k_k[i])
        kk = kk / jnp.maximum(jnp.sqrt(jnp.sum(kk * kk, -1, keepdims=True)), 1e-12)
        k_r = pk * (1 + (a_r - 1) * k_a[i])
        r_h, k_h, v_h = hd(pr), hd(k_r), hd(v_r)
        kk_f = kk.reshape(bsz, seq, D_R)
        y = hd(_rwkv_scan(pr, lw, k_r, v_r, kk_f, kk_f * a_r))
        mu_y = jnp.mean(y, -1, keepdims=True)
        var_y = jnp.mean(jnp.square(y - mu_y), -1, keepdims=True)
        y = ((y - mu_y) * lax.rsqrt(var_y + GN_EPS)).reshape(bsz, seq, D_R) * g_lnx[i] + b_lnx[i]
        bonus = jnp.sum(r_h * k_h * r_k[i], -1, keepdims=True) * v_h
        y = y + bonus.reshape(bsz, seq, D_R)
        y_b = mm3(y * g_r, w_ob[i])

        z = gates[..., :D_MODEL] * y_a + gates[..., D_MODEL:] * y_b
        x = _layer_norm(alpha * x + mm3(z, w_out[i]), ln1_g[i], ln1_b[i])

        x = _ffn(x, w_up[i], conv_w[i], conv_b[i], w_down[i], ln2_g[i], ln2_b[i], alpha)
    return x
```

```python
import functools

import jax
import jax.numpy as jnp
import numpy as np
from jax import lax
from jax.experimental import pallas as pl
from jax.experimental.pallas import tpu as pltpu

D_MODEL = 1024
DEPTH = 4
H_A = 8
DH_A = 128
D_CQ = 256
ROT_A = DH_A // 4
H_IDX = 8
D_IDX = 64
ROT_IDX = D_IDX // 4
TOPK_MAX = 256
ROPE_THETA = 500000.0
N_R = 64
H_R = D_MODEL // N_R
D_R = H_R * N_R
LORA_W = 64
LORA_A = 64
LORA_V = 32
LORA_G = 128
GN_EPS = 64e-5
D_FF = ((8 * D_MODEL // 3 + 127) // 128) * 128
LN_EPS = 1e-5

N_ATT = D_CQ + 2 * DH_A + D_IDX + H_IDX
N_GATE = 2 * D_MODEL
N_RWKV = 3 * D_R + LORA_W + LORA_A + LORA_G
N_IN0 = N_ATT + N_GATE + N_RWKV

LANES = 128
VMEM_LIMIT_BYTES = 48 * 1024 * 1024

N_ATT_PAD = 640
N_RW_PAD = 3584

ATTN_TILE = 256
SCAN_CHUNK = 64
SCAN_GROUP_TILES = 1
NEG_BIG = -1e30
INT_MIN = -(2 ** 31)


def _mm_kernel(x_ref, w_ref, o_ref):
    o_ref[...] = jnp.dot(x_ref[...].astype(jnp.bfloat16), w_ref[...],
                         preferred_element_type=jnp.float32).astype(o_ref.dtype)


def _pick_tile(n, cap):
    best = None
    for t in range(LANES, min(n, cap) + 1, LANES):
        if n % t == 0:
            best = t
    return best if best is not None else n


def _mm(x, w, out_dtype=jnp.float32, tm=None):
    m, k = x.shape
    n = w.shape[1]
    w = w.astype(jnp.bfloat16)
    if tm is None:
        tm = 1024 if k <= 1024 else 512
    tm = min(tm, m)
    tn = _pick_tile(n, 1024 if k <= 1024 else 512)
    return pl.pallas_call(
        _mm_kernel,
        grid=(m // tm, n // tn),
        in_specs=[pl.BlockSpec((tm, k), lambda i, j: (i, 0)),
                  pl.BlockSpec((k, tn), lambda i, j: (0, j))],
        out_specs=pl.BlockSpec((tm, tn), lambda i, j: (i, j)),
        out_shape=jax.ShapeDtypeStruct((m, n), out_dtype),
        compiler_params=pltpu.CompilerParams(
            dimension_semantics=("parallel", "arbitrary"),
            vmem_limit_bytes=VMEM_LIMIT_BYTES),
        name="mm",
    )(x, w)


def _gate_kernel(x_ref, w_ref, b_ref, o_ref):
    p = jnp.dot(x_ref[...].astype(jnp.bfloat16), w_ref[...], preferred_element_type=jnp.float32)
    o_ref[...] = jax.nn.sigmoid(p + b_ref[...])


def _mm_gate(x, w, bias):
    m, k = x.shape
    n = w.shape[1]
    tm = min(1024, m)
    tn = _pick_tile(n, 1024)
    return pl.pallas_call(
        _gate_kernel,
        grid=(m // tm, n // tn),
        in_specs=[pl.BlockSpec((tm, k), lambda i, j: (i, 0)),
                  pl.BlockSpec((k, tn), lambda i, j: (0, j)),
                  pl.BlockSpec((1, tn), lambda i, j: (0, j))],
        out_specs=pl.BlockSpec((tm, tn), lambda i, j: (i, j)),
        out_shape=jax.ShapeDtypeStruct((m, n), jnp.float32),
        compiler_params=pltpu.CompilerParams(
            dimension_semantics=("parallel", "arbitrary"),
            vmem_limit_bytes=VMEM_LIMIT_BYTES),
        name="mm_gate",
    )(x, w.astype(jnp.bfloat16), bias.reshape(1, n))


def _shift_rows(cur, prev_rows, k):
    rolled = pltpu.roll(cur, k, 0)
    row = lax.broadcasted_iota(jnp.int32, cur.shape, 0)
    for i in range(k):
        rolled = jnp.where(row == i, prev_rows[i:i + 1, :], rolled)
    return rolled


def _shift_mm_kernel(x_ref, w_ref, mu_ref, o_ref, carry_ref):
    si = pl.program_id(1)
    j = pl.program_id(2)
    tm = o_ref.shape[1]
    p = jnp.dot(x_ref[0].astype(jnp.bfloat16), w_ref[...], preferred_element_type=jnp.float32)
    @pl.when(si == 0)
    def _():
        carry_ref[j] = jnp.zeros(carry_ref.shape[1:], jnp.float32)

    prev = _shift_rows(p, carry_ref[j, 0:1, :], 1)
    carry_ref[j, 0:1, :] = p[tm - 1:tm, :]
    o_ref[0] = p + (prev - p) * mu_ref[...]


def _mm_shift(x, w, mu):
    b, s, k = x.shape
    n = w.shape[1]
    tm = min(1024, s)
    tn = 512 if n % 512 == 0 else _pick_tile(n, 1024)
    return pl.pallas_call(
        _shift_mm_kernel,
        grid=(b, s // tm, n // tn),
        in_specs=[pl.BlockSpec((1, tm, k), lambda bi, si, j: (bi, si, 0)),
                  pl.BlockSpec((k, tn), lambda bi, si, j: (0, j)),
                  pl.BlockSpec((1, tn), lambda bi, si, j: (0, j))],
        out_specs=pl.BlockSpec((1, tm, tn), lambda bi, si, j: (bi, si, j)),
        out_shape=jax.ShapeDtypeStruct((b, s, n), jnp.float32),
        scratch_shapes=[pltpu.VMEM((n // tn, 8, tn), jnp.float32)],
        compiler_params=pltpu.CompilerParams(
            dimension_semantics=("parallel", "arbitrary", "arbitrary"),
            vmem_limit_bytes=VMEM_LIMIT_BYTES),
        name="mm_shift",
    )(x, w.astype(jnp.bfloat16), mu.reshape(1, n))


def _ffn_kernel(x_ref, wg_ref, wv_ref, cw_ref, cb_ref, wd_ref, g_ref, b_ref, o_ref,
                xb_ref, acc_ref, carry_ref, *, alpha, eps):
    si = pl.program_id(1)
    c = pl.program_id(2)
    tm = o_ref.shape[1]

    @pl.when(c == 0)
    def _():
        xb_ref[...] = x_ref[0].astype(jnp.bfloat16)
        acc_ref[...] = jnp.zeros(acc_ref.shape, jnp.float32)

    xb = xb_ref[...]
    hg = jnp.dot(xb, wg_ref[...], preferred_element_type=jnp.float32)
    hv = jnp.dot(xb, wv_ref[...], preferred_element_type=jnp.float32)
    @pl.when(si == 0)
    def _():
        carry_ref[c] = jnp.zeros(carry_ref.shape[1:], jnp.float32)

    tail = carry_ref[c, 0:2, :]
    g1 = _shift_rows(hg, tail[1:2, :], 1)
    g2 = _shift_rows(hg, tail, 2)
    carry_ref[c, 0:2, :] = hg[tm - 2:tm, :]
    cw = cw_ref[...]
    conv = g2 * cw[0:1, :] + g1 * cw[1:2, :] + hg * cw[2:3, :] + cb_ref[...]
    act = jax.nn.silu(conv) * hv
    acc_ref[...] += jnp.dot(act.astype(jnp.bfloat16), wd_ref[...], preferred_element_type=jnp.float32)

    @pl.when(c == pl.num_programs(2) - 1)
    def _():
        y = alpha * x_ref[0] + acc_ref[...]
        mu = jnp.mean(y, axis=-1, keepdims=True)
        d = y - mu
        var = jnp.mean(d * d, axis=-1, keepdims=True)
        o_ref[0] = d * lax.rsqrt(var + eps) * g_ref[...] + b_ref[...]


def _ffn(x, w_up, conv_w, conv_b, w_down, ln_g, ln_b, alpha):
    b, s, d = x.shape
    dff = w_down.shape[0]
    tm = min(1024, s)
    tf = 256 if dff % 256 == 0 else LANES
    nc = dff // tf
    kern = functools.partial(_ffn_kernel, alpha=alpha, eps=LN_EPS)
    w_up = w_up.astype(jnp.bfloat16)
    return pl.pallas_call(
        kern,
        grid=(b, s // tm, nc),
        in_specs=[
            pl.BlockSpec((1, tm, d), lambda bi, si, c: (bi, si, 0)),
            pl.BlockSpec((d, tf), lambda bi, si, c: (0, c)),
            pl.BlockSpec((d, tf), lambda bi, si, c: (0, c + nc)),
            pl.BlockSpec((3, tf), lambda bi, si, c: (0, c)),
            pl.BlockSpec((1, tf), lambda bi, si, c: (0, c)),
            pl.BlockSpec((tf, d), lambda bi, si, c: (c, 0)),
            pl.BlockSpec((1, d), lambda bi, si, c: (0, 0)),
            pl.BlockSpec((1, d), lambda bi, si, c: (0, 0)),
        ],
        out_specs=pl.BlockSpec((1, tm, d), lambda bi, si, c: (bi, si, 0)),
        out_shape=jax.ShapeDtypeStruct((b, s, d), jnp.float32),
        scratch_shapes=[
            pltpu.VMEM((tm, d), jnp.bfloat16),
            pltpu.VMEM((tm, d), jnp.float32),
            pltpu.VMEM((nc, 8, tf), jnp.float32),
        ],
        compiler_params=pltpu.CompilerParams(
            dimension_semantics=("parallel", "arbitrary", "arbitrary"),
            vmem_limit_bytes=VMEM_LIMIT_BYTES),
        name="ffn",
    )(x, w_up, w_up, conv_w, conv_b.reshape(1, dff), w_down.astype(jnp.bfloat16),
      ln_g.reshape(1, d), ln_b.reshape(1, d))


def _sortable_key(score):
    score = jnp.where(score == 0.0, 0.0, score)
    bits = pltpu.bitcast(score, jnp.int32)
    return jnp.where(bits < 0, bits ^ jnp.int32(0x7FFFFFFF), bits)


def _attn_kernel(qt_ref, qit_ref, wi_ref, k_ref, vt_ref, ki_ref, o_ref,
                 key_ref, jthr_ref, m_ref, l_ref, acc_ref, *, t, nh, topk, scale):
    qb = pl.program_id(1)
    n_kt = qb + 1
    krow = lax.broadcasted_iota(jnp.int32, (t, t), 0)
    qcol = lax.broadcasted_iota(jnp.int32, (t, t), 1)

    qit = qit_ref[0, 0]
    wi = wi_ref[0, 0]

    def score_tile(kt, carry):
        ki_t = ki_ref[0, pl.ds(kt * t, t), :]
        rel = jnp.dot(ki_t, qit, preferred_element_type=jnp.float32)
        rel = jnp.maximum(rel, 0.0) * wi
        score = rel[:, 0:t]
        for h in range(1, nh):
            score = score + rel[:, h * t:(h + 1) * t]
        score = jnp.where(jnp.logical_and(kt == qb, krow > qcol), -jnp.inf, score)
        key_ref[kt] = _sortable_key(score)
        return carry

    lax.fori_loop(0, n_kt, score_tile, 0)

    def count(pred_fn):
        def body(kt, acc):
            c = jnp.where(pred_fn(key_ref[kt], kt), 1, 0).astype(jnp.int32)
            return acc + jnp.sum(c.reshape(t // 8, 8, t), axis=0)
        acc = lax.fori_loop(0, n_kt, body, jnp.zeros((8, t), jnp.int32))
        return jnp.sum(acc, axis=0, keepdims=True)

    def count_ge(cand):
        return count(lambda kk, kt: kk >= cand)

    thr0 = jnp.where(count_ge(jnp.zeros((1, t), jnp.int32)) >= topk,
                     jnp.int32(0), jnp.int32(INT_MIN))

    def bit_step(i, thr):
        cand = thr | jnp.left_shift(jnp.int32(1), 30 - i)
        return jnp.where(count_ge(cand) >= topk, cand, thr)

    thr = lax.fori_loop(0, 31, bit_step, thr0)
    jthr_ref[...] = jnp.full(jthr_ref.shape, 2 ** 30, jnp.int32)
    n_ge = count_ge(thr)

    @pl.when(jnp.max(n_ge) > topk)
    def _():
        need = topk - count(lambda kk, kt: kk > thr)
        idx_bits = int(np.ceil(np.log2(key_ref.shape[0] * t)))

        def idx_step(i, j):
            cand = j | jnp.left_shift(jnp.int32(1), idx_bits - 1 - i)
            below = count(lambda kk, kt: jnp.logical_and(kk == thr, krow + kt * t < cand))
            return jnp.where(below < need, cand, j)

        jthr_ref[...] = lax.fori_loop(0, idx_bits, idx_step, jnp.zeros((1, t), jnp.int32))

    qt = qt_ref[0, 0]
    m_ref[...] = jnp.full(m_ref.shape, NEG_BIG, jnp.float32)
    l_ref[...] = jnp.zeros(l_ref.shape, jnp.float32)
    acc_ref[...] = jnp.zeros(acc_ref.shape, jnp.float32)
    jthr = jthr_ref[...]

    def attn_tile(kt, carry):
        kk = key_ref[kt]
        gk = krow + kt * t
        sel = jnp.logical_or(kk > thr, jnp.logical_and(kk == thr, gk <= jthr))
        sel = jnp.logical_and(sel, gk <= qcol + qb * t)
        bias = jnp.where(sel, 0.0, NEG_BIG)
        k_t = k_ref[0, pl.ds(kt * t, t), :]
        vt_t = vt_ref[0, kt]
        s_all = jnp.dot(k_t, qt, preferred_element_type=jnp.float32)
        for h in range(nh):
            s = s_all[:, h * t:(h + 1) * t] * scale + bias
            m_old = m_ref[h:h + 1, :]
            m_new = jnp.maximum(m_old, jnp.max(s, axis=0, keepdims=True))
            alpha = jnp.exp(m_old - m_new)
            p = jnp.exp(s - m_new)
            l_ref[h:h + 1, :] = l_ref[h:h + 1, :] * alpha + jnp.sum(p, axis=0, keepdims=True)
            acc_ref[h] = acc_ref[h] * alpha + jnp.dot(
                vt_t, p.astype(jnp.bfloat16), preferred_element_type=jnp.float32)
            m_ref[h:h + 1, :] = m_new
        return carry

    lax.fori_loop(0, n_kt, attn_tile, 0)
    dh = acc_ref.shape[1]
    for h in range(nh):
        out = acc_ref[h] / l_ref[h:h + 1, :]
        o_ref[0, :, h * dh:(h + 1) * dh] = out.T.astype(o_ref.dtype)


def _attention(q, qi, wi, k, v, ki, topk):
    b, s, nh, dh = q.shape
    di = qi.shape[3]
    t = min(ATTN_TILE, s)
    nq = s // t
    bf = jnp.bfloat16
    qt = q.astype(bf).reshape(b, nq, t, nh, dh).transpose(0, 1, 4, 3, 2).reshape(b, nq, dh, nh * t)
    qit = qi.astype(bf).reshape(b, nq, t, nh, di).transpose(0, 1, 4, 3, 2).reshape(b, nq, di, nh * t)
    wit = wi.reshape(b, nq, t, nh).transpose(0, 1, 3, 2).reshape(b, nq, 1, nh * t)
    vt = v.astype(bf).reshape(b, nq, t, dh).swapaxes(2, 3)
    kern = functools.partial(_attn_kernel, t=t, nh=nh, topk=topk, scale=float(dh) ** -0.5)
    return pl.pallas_call(
        kern,
        grid=(b, nq),
        in_specs=[
            pl.BlockSpec((1, 1, dh, nh * t), lambda bi, qb: (bi, qb, 0, 0)),
            pl.BlockSpec((1, 1, di, nh * t), lambda bi, qb: (bi, qb, 0, 0)),
            pl.BlockSpec((1, 1, 1, nh * t), lambda bi, qb: (bi, qb, 0, 0)),
            pl.BlockSpec((1, s, dh), lambda bi, qb: (bi, 0, 0)),
            pl.BlockSpec((1, nq, dh, t), lambda bi, qb: (bi, 0, 0, 0)),
            pl.BlockSpec((1, s, di), lambda bi, qb: (bi, 0, 0)),
        ],
        out_specs=pl.BlockSpec((1, t, nh * dh), lambda bi, qb: (bi, qb, 0)),
        out_shape=jax.ShapeDtypeStruct((b, s, nh * dh), jnp.bfloat16),
        scratch_shapes=[
            pltpu.VMEM((nq, t, t), jnp.int32),
            pltpu.VMEM((1, t), jnp.int32),
            pltpu.VMEM((nh, t), jnp.float32),
            pltpu.VMEM((nh, t), jnp.float32),
            pltpu.VMEM((nh, dh, t), jnp.float32),
        ],
        compiler_params=pltpu.CompilerParams(
            dimension_semantics=("parallel", "arbitrary"),
            vmem_limit_bytes=VMEM_LIMIT_BYTES),
        name="dsa_attention",
    )(qt, qit, wit, k.astype(bf), vt, ki.astype(bf))


def _bdot(a, b, dims=(((1,), (0,)), ((), ()))):
    return lax.dot_general(a.astype(jnp.bfloat16), b.astype(jnp.bfloat16), dims,
                           preferred_element_type=jnp.float32)


_NT = (((1,), (1,)), ((), ()))
_TN = (((0,), (0,)), ((), ()))


def _scan_kernel(r_ref, lw_ref, k_ref, v_ref, kk_ref, b_ref, y_ref, s_ref, *, chunk):
    @pl.when(pl.program_id(1) == 0)
    def _():
        s_ref[...] = jnp.zeros(s_ref.shape, jnp.float32)

    n_tiles = s_ref.shape[0]
    hl = LANES // 2
    gt = SCAN_GROUP_TILES
    rows = 2 * gt * chunk
    row = lax.broadcasted_iota(jnp.int32, (rows, rows), 0)
    col = lax.broadcasted_iota(jnp.int32, (rows, rows), 1)
    same = (row // chunk) == (col // chunk)
    incl = jnp.logical_and(same, row >= col)
    strict = jnp.logical_and(same, row > col)
    r1 = lax.broadcasted_iota(jnp.int32, (chunk, chunk), 0)
    c1 = lax.broadcasted_iota(jnp.int32, (chunk, chunk), 1)
    tril = jnp.where(r1 >= c1, 1.0, 0.0).astype(jnp.float32)
    lane = lax.broadcasted_iota(jnp.int32, (1, LANES), 1)
    even = lane < hl
    pr_ = lax.broadcasted_iota(jnp.int32, (LANES, LANES), 0)
    pc_ = lax.broadcasted_iota(jnp.int32, (LANES, LANES), 1)
    pair_bd = (pr_ // hl) == (pc_ // hl)

    lw = lw_ref[0]
    c = jnp.dot(tril, lw, precision=lax.Precision.HIGHEST,
                preferred_element_type=jnp.float32)
    e_neg = jnp.exp(-c)
    at = -kk_ref[0] * jnp.exp(c - lw)
    rt = r_ref[0] * jnp.exp(c)
    bt = b_ref[0] * e_neg
    kt = k_ref[0] * e_neg
    v = v_ref[0]
    p_last = jnp.exp(c[chunk - 1:chunk, :])
    bp = bt * p_last
    kp = kt * p_last

    def tile(x, t):
        return x[:, t * LANES:(t + 1) * LANES]

    def stack_masked(x, t0):
        parts = []
        for t in range(t0, t0 + gt):
            xt = tile(x, t)
            parts += [jnp.where(even, xt, 0.0), jnp.where(even, 0.0, xt)]
        return jnp.concatenate(parts, axis=0)

    def stack_plain(x, t0):
        return jnp.concatenate([tile(x, t) for t in range(t0, t0 + gt) for _ in range(2)], axis=0)

    steps = max(1, int(np.ceil(np.log2(chunk))))
    groups = list(range(0, n_tiles, gt))
    at_s = [stack_masked(at, t0) for t0 in groups]
    v_s = [stack_masked(v, t0) for t0 in groups]
    a_all = [_bdot(jnp.concatenate([at_s[g], stack_masked(rt, t0)], axis=0),
                   jnp.concatenate([stack_plain(bt, t0), stack_plain(kt, t0)], axis=0), _NT)
             for g, t0 in enumerate(groups)]
    mpow = [jnp.where(strict, a[:rows, :rows], 0.0) for a in a_all]
    a_ak = [jnp.where(strict, a[:rows, rows:], 0.0) for a in a_all]
    a_rb = [jnp.where(incl, a[rows:, :rows], 0.0) for a in a_all]
    a_rk = [jnp.where(incl, a[rows:, rows:], 0.0) for a in a_all]

    xs = [jnp.concatenate([at_s[g], _bdot(a_ak[g], v_s[g])], axis=1) for g in range(len(groups))]
    for i in range(steps):
        xs = [x + _bdot(m_, x) for x, m_ in zip(xs, mpow)]
        if i + 1 < steps:
            mpow = [_bdot(m_, m_) for m_ in mpow]
    qys = [_bdot(a_rb[g], xs[g]) for g in range(len(groups))]
    y0s = [qys[g][:, LANES:] + _bdot(a_rk[g], v_s[g]) for g in range(len(groups))]

    for g, t0 in enumerate(groups):
        x = xs[g]
        y0_s = y0s[g]
        qh_s = qys[g][:, :LANES]
        for j in range(gt):
            t = t0 + j
            lo, mid, hi = 2 * j * chunk, (2 * j + 1) * chunk, (2 * j + 2) * chunk
            unstack = lambda z: z[lo:mid] + z[mid:hi]
            x_t = unstack(x)
            qh_t = tile(rt, t) + unstack(qh_s)
            y0_t = unstack(y0_s)
            s0 = s_ref[t]
            gw = _bdot(x_t, tile(bp, t), _TN)
            g_t = jnp.where(pair_bd, gw[:LANES], 0.0)
            h_t = jnp.where(pair_bd, gw[LANES:] + _bdot(tile(v, t), tile(kp, t), _TN), 0.0)
            y_ref[0, :, t * LANES:(t + 1) * LANES] = y0_t + _bdot(qh_t, s0, _NT)
            s_ref[t] = s0 * tile(p_last, t) + _bdot(s0, g_t) + h_t


def _rwkv_scan(r, lw, k, v, kk, b):
    bsz, s, d = r.shape
    chunk = min(SCAN_CHUNK, s)
    spec = pl.BlockSpec((1, chunk, d), lambda bi, ci: (bi, ci, 0))
    kern = functools.partial(_scan_kernel, chunk=chunk)
    return pl.pallas_call(
        kern,
        grid=(bsz, s // chunk),
        in_specs=[spec] * 6,
        out_specs=spec,
        out_shape=jax.ShapeDtypeStruct((bsz, s, d), jnp.float32),
        scratch_shapes=[pltpu.VMEM((d // LANES, LANES, LANES), jnp.float32)],
        compiler_params=pltpu.CompilerParams(
            dimension_semantics=("parallel", "arbitrary"),
            vmem_limit_bytes=VMEM_LIMIT_BYTES),
        name="rwkv7_scan",
    )(r, lw, k, v, kk, b)


def _head_sum(z, seg):
    hi = z.astype(jnp.bfloat16)
    lo = (z - hi.astype(jnp.float32)).astype(jnp.bfloat16)
    outs = []
    for t in range(z.shape[1] // LANES):
        sl = slice(t * LANES, (t + 1) * LANES)
        outs.append(jnp.dot(hi[:, sl], seg, preferred_element_type=jnp.float32)
                    + jnp.dot(lo[:, sl], seg, preferred_element_type=jnp.float32))
    return jnp.concatenate(outs, axis=1)


def _post_kernel(y_ref, r_ref, k_ref, v_ref, g_ref, o_ref, gate_ref, x_ref,
                 glnx_ref, blnx_ref, rk_ref, lng_ref, lnb_ref, seg_ref, wob_ref, woa_ref, wout_ref,
                 out_ref, *, alpha):
    seg = seg_ref[...]
    d_model = x_ref.shape[1]
    inv_n = 1.0 / N_R
    y = y_ref[...]
    dy = y - _head_sum(y, seg) * inv_n
    var = _head_sum(dy * dy, seg) * inv_n
    yn = dy * lax.rsqrt(var + GN_EPS) * glnx_ref[...] + blnx_ref[...]
    bonus = _head_sum(r_ref[...] * k_ref[...] * rk_ref[...], seg) * v_ref[...]
    yb_in = ((yn + bonus) * g_ref[...]).astype(jnp.bfloat16)
    y_b = jnp.dot(yb_in, wob_ref[...], preferred_element_type=jnp.float32)
    y_a = jnp.dot(o_ref[...], woa_ref[...], preferred_element_type=jnp.float32)
    gates = gate_ref[...]
    z = gates[:, :d_model] * y_a + gates[:, d_model:] * y_b
    t = alpha * x_ref[...] + jnp.dot(z.astype(jnp.bfloat16), wout_ref[...],
                                     preferred_element_type=jnp.float32)
    mu = jnp.mean(t, axis=-1, keepdims=True)
    dt = t - mu
    vt = jnp.mean(dt * dt, axis=-1, keepdims=True)
    out_ref[...] = dt * lax.rsqrt(vt + LN_EPS) * lng_ref[...] + lnb_ref[...]


def _mixer_post(y, r, k, v, g, o, gates, x, g_lnx, b_lnx, r_k, ln_g, ln_b, w_ob, w_oa, w_out, alpha):
    m, d = x.shape
    tm = min(256, m)
    hl = LANES // 2
    seg = (jnp.arange(LANES)[:, None] // hl == jnp.arange(LANES)[None, :] // hl).astype(jnp.bfloat16)
    row = lambda n: pl.BlockSpec((tm, n), lambda i: (i, 0))
    vec = pl.BlockSpec((1, d), lambda i: (0, 0))
    full = lambda a, b_: pl.BlockSpec((a, b_), lambda i: (0, 0))
    bf = jnp.bfloat16
    return pl.pallas_call(
        functools.partial(_post_kernel, alpha=alpha),
        grid=(m // tm,),
        in_specs=[row(d), row(d), row(d), row(d), row(d), row(d), row(2 * d), row(d),
                  vec, vec, vec, vec, vec, full(LANES, LANES), full(d, d), full(d, d), full(d, d)],
        out_specs=row(d),
        out_shape=jax.ShapeDtypeStruct((m, d), jnp.float32),
        compiler_params=pltpu.CompilerParams(
            dimension_semantics=("parallel",),
            vmem_limit_bytes=VMEM_LIMIT_BYTES),
        name="mixer_post",
    )(y, r, k, v, g, o, gates, x,
      g_lnx.reshape(1, d), b_lnx.reshape(1, d), r_k.reshape(1, d), ln_g.reshape(1, d), ln_b.reshape(1, d),
      seg, w_ob.astype(bf), w_oa.astype(bf), w_out.astype(bf))


def _layer_norm(x, g, b, eps=LN_EPS):
    mu = jnp.mean(x, -1, keepdims=True)
    var = jnp.mean(jnp.square(x - mu), -1, keepdims=True)
    return (x - mu) * lax.rsqrt(var + eps) * g + b


def _rms_norm(x, g, eps=1e-6):
    return x * lax.rsqrt(jnp.mean(jnp.square(x), -1, keepdims=True) + eps) * g


def _rope_tables(positions, rot_dim):
    inv_freq = jnp.power(jnp.float32(ROPE_THETA), -jnp.arange(0, rot_dim, 2, dtype=jnp.float32) / rot_dim)
    ang = positions.astype(jnp.float32)[..., None] * inv_freq
    return jnp.cos(ang)[:, :, None, :], jnp.sin(ang)[:, :, None, :]


def _partial_rope(x, cos, sin):
    half = cos.shape[-1]
    x1, x2, rest = x[..., :half], x[..., half:2 * half], x[..., 2 * half:]
    return jnp.concatenate([x1 * cos - x2 * sin, x2 * cos + x1 * sin, rest], -1)


def kernel(x, positions, w_in0, w_in_rest, b_gate, g_cq, w_uq, w_iq, g_ik, b_ik, w_oa, mu_rwkv, mu_vres, w0, w2, a0, a2, v0, v2, g2, k_k, k_a, r_k, g_lnx, b_lnx, w_ob, w_out, ln1_g, ln1_b, w_up, conv_w, conv_b, w_down, ln2_g, ln2_b):
    bsz, seq, d = x.shape
    m = bsz * seq
    alpha = (2 * DEPTH) ** 0.25
    topk = min(TOPK_MAX, seq // 4)
    rope_a = _rope_tables(positions, ROT_A)
    rope_i = _rope_tables(positions, ROT_IDX)
    v_first = None
    mm3 = lambda t, w: _mm(t.reshape(m, t.shape[-1]), w).reshape(bsz, seq, w.shape[1])

    for i in range(DEPTH):
        w_in = w_in0 if i == 0 else w_in_rest[i - 1]
        zcols = lambda n: jnp.zeros((d, n), w_in.dtype)
        w_att = jnp.concatenate([w_in[:, :N_ATT], zcols(N_ATT_PAD - N_ATT)], axis=1)
        n_rw = w_in.shape[1] - (N_ATT + N_GATE)
        w_rw = jnp.concatenate([w_in[:, N_ATT + N_GATE:], zcols(N_RW_PAD - n_rw)], axis=1)
        mu_parts = [mu_rwkv[i]] + ([mu_vres[i - 1]] if i > 0 else [])
        mu_rw = jnp.concatenate(mu_parts + [jnp.zeros((N_RW_PAD - n_rw,), jnp.float32)])
        p_att = mm3(x, w_att)
        gates = _mm_gate(x.reshape(m, d), w_in[:, N_ATT:N_ATT + N_GATE], b_gate[i]).reshape(bsz, seq, N_GATE)
        p_rw = _mm_shift(x, w_rw, mu_rw)

        cq_raw = p_att[..., :D_CQ]
        k_raw = p_att[..., D_CQ:D_CQ + DH_A]
        v_raw = p_att[..., D_CQ + DH_A:D_CQ + 2 * DH_A]
        ik_raw = p_att[..., D_CQ + 2 * DH_A:D_CQ + 2 * DH_A + D_IDX]
        iw_raw = p_att[..., D_CQ + 2 * DH_A + D_IDX:N_ATT]
        c_q = _rms_norm(cq_raw, g_cq[i])
        q = _partial_rope(mm3(c_q, w_uq[i]).reshape(bsz, seq, H_A, DH_A), *rope_a)
        k_att = _partial_rope(k_raw[:, :, None, :], *rope_a)[:, :, 0]
        qi = _partial_rope(mm3(c_q, w_iq[i]).reshape(bsz, seq, H_IDX, D_IDX), *rope_i)
        ki = _partial_rope(_layer_norm(ik_raw, g_ik[i], b_ik[i])[:, :, None, :], *rope_i)[:, :, 0]
        wi = iw_raw * (H_IDX ** -0.5 * D_IDX ** -0.5)
        o = _attention(q, qi, wi, k_att, v_raw, ki, topk)

        pr = p_rw[..., :D_R]
        pk = p_rw[..., D_R:2 * D_R]
        pv = p_rw[..., 2 * D_R:3 * D_R]
        pw = p_rw[..., 3 * D_R:3 * D_R + LORA_W]
        pa = p_rw[..., 3 * D_R + LORA_W:3 * D_R + LORA_W + LORA_A]
        pg = p_rw[..., 3 * D_R + LORA_W + LORA_A:N_RWKV]
        if i == 0:
            v_first = pv
            v_r = pv
        else:
            p_vr = p_rw[..., N_RWKV:N_RWKV + LORA_V]
            v_r = pv + (v_first - pv) * jax.nn.sigmoid(v0[i - 1] + mm3(p_vr, v2[i - 1]))
        w_dec = -jax.nn.softplus(-(w0[i] + mm3(jnp.tanh(pw), w2[i]))) - 0.5
        lw = -jnp.exp(w_dec)
        a_r = jax.nn.sigmoid(a0[i] + mm3(pa, a2[i]))
        g_r = mm3(jax.nn.sigmoid(pg), g2[i])
        hd = lambda t_: t_.reshape(bsz, seq, H_R, N_R)
        kk = hd(pk * k_k[i])
        kk = kk / jnp.maximum(jnp.sqrt(jnp.sum(kk * kk, -1, keepdims=True)), 1e-12)
        k_r = pk * (1 + (a_r - 1) * k_a[i])
        kk_f = kk.reshape(bsz, seq, D_R)
        y = _rwkv_scan(pr, lw, k_r, v_r, kk_f, kk_f * a_r)

        f2 = lambda t_: t_.reshape(m, t_.shape[-1])
        x = _mixer_post(f2(y), f2(p_rw), f2(k_r), f2(v_r), f2(g_r), f2(o), f2(gates), f2(x),
                        g_lnx[i], b_lnx[i], r_k[i], ln1_g[i], ln1_b[i],
                        w_ob[i], w_oa[i], w_out[i], alpha).reshape(bsz, seq, d)

        x = _ffn(x, w_up[i], conv_w[i], conv_b[i], w_down[i], ln2_g[i], ln2_b[i], alpha)
    return x
```

```python
import functools

import jax
import jax.numpy as jnp
import numpy as np
from jax import lax
from jax.experimental import pallas as pl
from jax.experimental.pallas import tpu as pltpu

D_MODEL = 1024
DEPTH = 4
H_A = 8
DH_A = 128
D_CQ = 256
ROT_A = DH_A // 4
H_IDX = 8
D_IDX = 64
ROT_IDX = D_IDX // 4
TOPK_MAX = 256
ROPE_THETA = 500000.0
N_R = 64
H_R = D_MODEL // N_R
D_R = H_R * N_R
LORA_W = 64
LORA_A = 64
LORA_V = 32
LORA_G = 128
GN_EPS = 64e-5
D_FF = ((8 * D_MODEL // 3 + 127) // 128) * 128
LN_EPS = 1e-5

N_ATT = D_CQ + 2 * DH_A + D_IDX + H_IDX
N_GATE = 2 * D_MODEL
N_RWKV = 3 * D_R + LORA_W + LORA_A + LORA_G
N_IN0 = N_ATT + N_GATE + N_RWKV

LANES = 128
SUBLANES = 8
LOG2E = 1.4426950408889634
VMEM_LIMIT_BYTES = 48 * 1024 * 1024

N_ATT_PAD = 640
N_RW_PAD = 3584

ATTN_TILE = 256
SCAN_CHUNK = 64
SCAN_GROUP_TILES = 1
NEG_BIG = -1e30
INT_MIN = -(2 ** 31)


def _mm_kernel(x_ref, w_ref, o_ref):
    o_ref[...] = jnp.dot(x_ref[...].astype(jnp.bfloat16), w_ref[...],
                         preferred_element_type=jnp.float32).astype(o_ref.dtype)


def _pick_tile(n, cap):
    best = None
    for t in range(LANES, min(n, cap) + 1, LANES):
        if n % t == 0:
            best = t
    return best if best is not None else n


def _mm(x, w, out_dtype=jnp.float32, tm=None):
    m, k = x.shape
    n = w.shape[1]
    w = w.astype(jnp.bfloat16)
    if tm is None:
        tm = 1024 if k <= 1024 else 512
    tm = min(tm, m)
    tn = _pick_tile(n, 1024 if k <= 1024 else 512)
    return pl.pallas_call(
        _mm_kernel,
        grid=(m // tm, n // tn),
        in_specs=[pl.BlockSpec((tm, k), lambda i, j: (i, 0)),
                  pl.BlockSpec((k, tn), lambda i, j: (0, j))],
        out_specs=pl.BlockSpec((tm, tn), lambda i, j: (i, j)),
        out_shape=jax.ShapeDtypeStruct((m, n), out_dtype),
        compiler_params=pltpu.CompilerParams(
            dimension_semantics=("parallel", "arbitrary"),
            vmem_limit_bytes=VMEM_LIMIT_BYTES),
        name="mm",
    )(x, w)


def _gate_kernel(x_ref, w_ref, b_ref, o_ref):
    p = jnp.dot(x_ref[...].astype(jnp.bfloat16), w_ref[...], preferred_element_type=jnp.float32)
    o_ref[...] = jax.nn.sigmoid(p + b_ref[...])


def _mm_gate(x, w, bias):
    m, k = x.shape
    n = w.shape[1]
    tm = min(1024, m)
    tn = _pick_tile(n, 1024)
    return pl.pallas_call(
        _gate_kernel,
        grid=(m // tm, n // tn),
        in_specs=[pl.BlockSpec((tm, k), lambda i, j: (i, 0)),
                  pl.BlockSpec((k, tn), lambda i, j: (0, j)),
                  pl.BlockSpec((1, tn), lambda i, j: (0, j))],
        out_specs=pl.BlockSpec((tm, tn), lambda i, j: (i, j)),
        out_shape=jax.ShapeDtypeStruct((m, n), jnp.float32),
        compiler_params=pltpu.CompilerParams(
            dimension_semantics=("parallel", "arbitrary"),
            vmem_limit_bytes=VMEM_LIMIT_BYTES),
        name="mm_gate",
    )(x, w.astype(jnp.bfloat16), bias.reshape(1, n))


def _shift_rows(cur, prev_rows, k):
    rolled = pltpu.roll(cur, k, 0)
    row = lax.broadcasted_iota(jnp.int32, cur.shape, 0)
    for i in range(k):
        rolled = jnp.where(row == i, prev_rows[i:i + 1, :], rolled)
    return rolled


def _shift_mm_kernel(x_ref, w_ref, mu_ref, o_ref, carry_ref):
    si = pl.program_id(1)
    j = pl.program_id(2)
    tm = o_ref.shape[1]
    p = jnp.dot(x_ref[0].astype(jnp.bfloat16), w_ref[...], preferred_element_type=jnp.float32)
    @pl.when(si == 0)
    def _():
        carry_ref[j] = jnp.zeros(carry_ref.shape[1:], jnp.float32)

    prev = _shift_rows(p, carry_ref[j, 0:1, :], 1)
    carry_ref[j, 0:1, :] = p[tm - 1:tm, :]
    o_ref[0] = p + (prev - p) * mu_ref[...]


def _mm_shift(x, w, mu):
    b, s, k = x.shape
    n = w.shape[1]
    tm = min(1024, s)
    tn = 512 if n % 512 == 0 else _pick_tile(n, 1024)
    return pl.pallas_call(
        _shift_mm_kernel,
        grid=(b, s // tm, n // tn),
        in_specs=[pl.BlockSpec((1, tm, k), lambda bi, si, j: (bi, si, 0)),
                  pl.BlockSpec((k, tn), lambda bi, si, j: (0, j)),
                  pl.BlockSpec((1, tn), lambda bi, si, j: (0, j))],
        out_specs=pl.BlockSpec((1, tm, tn), lambda bi, si, j: (bi, si, j)),
        out_shape=jax.ShapeDtypeStruct((b, s, n), jnp.float32),
        scratch_shapes=[pltpu.VMEM((n // tn, 8, tn), jnp.float32)],
        compiler_params=pltpu.CompilerParams(
            dimension_semantics=("parallel", "arbitrary", "arbitrary"),
            vmem_limit_bytes=VMEM_LIMIT_BYTES),
        name="mm_shift",
    )(x, w.astype(jnp.bfloat16), mu.reshape(1, n))


def _ffn_kernel(x_ref, wg_ref, wv_ref, cw_ref, cb_ref, wd_ref, g_ref, b_ref, o_ref,
                xb_ref, acc_ref, carry_ref, *, alpha, eps):
    si = pl.program_id(1)
    c = pl.program_id(2)
    tm = o_ref.shape[1]

    @pl.when(c == 0)
    def _():
        xb_ref[...] = x_ref[0].astype(jnp.bfloat16)
        acc_ref[...] = jnp.zeros(acc_ref.shape, jnp.float32)

    xb = xb_ref[...]
    hg = jnp.dot(xb, wg_ref[...], preferred_element_type=jnp.float32)
    hv = jnp.dot(xb, wv_ref[...], preferred_element_type=jnp.float32)
    @pl.when(si == 0)
    def _():
        carry_ref[c] = jnp.zeros(carry_ref.shape[1:], jnp.float32)

    tail = carry_ref[c, 0:2, :]
    g1 = _shift_rows(hg, tail[1:2, :], 1)
    g2 = _shift_rows(hg, tail, 2)
    carry_ref[c, 0:2, :] = hg[tm - 2:tm, :]
    cw = cw_ref[...]
    conv = g2 * cw[0:1, :] + g1 * cw[1:2, :] + hg * cw[2:3, :] + cb_ref[...]
    act = jax.nn.silu(conv) * hv
    acc_ref[...] += jnp.dot(act.astype(jnp.bfloat16), wd_ref[...], preferred_element_type=jnp.float32)

    @pl.when(c == pl.num_programs(2) - 1)
    def _():
        y = alpha * x_ref[0] + acc_ref[...]
        mu = jnp.mean(y, axis=-1, keepdims=True)
        d = y - mu
        var = jnp.mean(d * d, axis=-1, keepdims=True)
        o_ref[0] = d * lax.rsqrt(var + eps) * g_ref[...] + b_ref[...]


def _ffn(x, w_up, conv_w, conv_b, w_down, ln_g, ln_b, alpha):
    b, s, d = x.shape
    dff = w_down.shape[0]
    tm = min(1024, s)
    tf = 256 if dff % 256 == 0 else LANES
    nc = dff // tf
    kern = functools.partial(_ffn_kernel, alpha=alpha, eps=LN_EPS)
    w_up = w_up.astype(jnp.bfloat16)
    return pl.pallas_call(
        kern,
        grid=(b, s // tm, nc),
        in_specs=[
            pl.BlockSpec((1, tm, d), lambda bi, si, c: (bi, si, 0)),
            pl.BlockSpec((d, tf), lambda bi, si, c: (0, c)),
            pl.BlockSpec((d, tf), lambda bi, si, c: (0, c + nc)),
            pl.BlockSpec((3, tf), lambda bi, si, c: (0, c)),
            pl.BlockSpec((1, tf), lambda bi, si, c: (0, c)),
            pl.BlockSpec((tf, d), lambda bi, si, c: (c, 0)),
            pl.BlockSpec((1, d), lambda bi, si, c: (0, 0)),
            pl.BlockSpec((1, d), lambda bi, si, c: (0, 0)),
        ],
        out_specs=pl.BlockSpec((1, tm, d), lambda bi, si, c: (bi, si, 0)),
        out_shape=jax.ShapeDtypeStruct((b, s, d), jnp.float32),
        scratch_shapes=[
            pltpu.VMEM((tm, d), jnp.bfloat16),
            pltpu.VMEM((tm, d), jnp.float32),
            pltpu.VMEM((nc, 8, tf), jnp.float32),
        ],
        compiler_params=pltpu.CompilerParams(
            dimension_semantics=("parallel", "arbitrary", "arbitrary"),
            vmem_limit_bytes=VMEM_LIMIT_BYTES),
        name="ffn",
    )(x, w_up, w_up, conv_w, conv_b.reshape(1, dff), w_down.astype(jnp.bfloat16),
      ln_g.reshape(1, d), ln_b.reshape(1, d))


def _sortable_key(score):
    score = jnp.where(score == 0.0, 0.0, score)
    bits = pltpu.bitcast(score, jnp.int32)
    return jnp.where(bits < 0, bits ^ jnp.int32(0x7FFFFFFF), bits)


def _attn_kernel(qt_ref, qit_ref, wi_ref, k_ref, vt_ref, ki_ref, o_ref,
                 key_ref, jthr_ref, m_ref, acc_ref, *, t, nh, topk):
    qb = pl.program_id(1)
    n_kt = qb + 1
    krow = lax.broadcasted_iota(jnp.int32, (t, t), 0)
    qcol = lax.broadcasted_iota(jnp.int32, (t, t), 1)

    qit = qit_ref[0, 0]
    wi = wi_ref[0, 0]

    def score_tile(kt, carry):
        ki_t = ki_ref[0, pl.ds(kt * t, t), :]
        rel = jnp.dot(ki_t, qit, preferred_element_type=jnp.float32)
        rel = jnp.maximum(rel, 0.0) * wi
        score = rel[:, 0:t]
        for h in range(1, nh):
            score = score + rel[:, h * t:(h + 1) * t]
        score = jnp.where(jnp.logical_and(kt == qb, krow > qcol), -jnp.inf, score)
        key_ref[kt] = _sortable_key(score)
        return carry

    lax.fori_loop(0, n_kt, score_tile, 0)

    def count(pred_fn):
        def body(kt, acc):
            c = jnp.where(pred_fn(key_ref[kt], kt), 1, 0).astype(jnp.int32)
            return acc + jnp.sum(c.reshape(t // 8, 8, t), axis=0)
        acc = lax.fori_loop(0, n_kt, body, jnp.zeros((8, t), jnp.int32))
        return jnp.sum(acc, axis=0, keepdims=True)

    def count_ge(cand):
        return count(lambda kk, kt: kk >= cand)

    thr0 = jnp.where(count_ge(jnp.zeros((1, t), jnp.int32)) >= topk,
                     jnp.int32(0), jnp.int32(INT_MIN))

    def bit_step(i, thr):
        cand = thr | jnp.left_shift(jnp.int32(1), 30 - i)
        return jnp.where(count_ge(cand) >= topk, cand, thr)

    thr = lax.fori_loop(0, 31, bit_step, thr0)
    jthr_ref[...] = jnp.full(jthr_ref.shape, 2 ** 30, jnp.int32)
    n_ge = count_ge(thr)

    @pl.when(jnp.max(n_ge) > topk)
    def _():
        need = topk - count(lambda kk, kt: kk > thr)
        idx_bits = int(np.ceil(np.log2(key_ref.shape[0] * t)))

        def idx_step(i, j):
            cand = j | jnp.left_shift(jnp.int32(1), idx_bits - 1 - i)
            below = count(lambda kk, kt: jnp.logical_and(kk == thr, krow + kt * t < cand))
            return jnp.where(below < need, cand, j)

        jthr_ref[...] = lax.fori_loop(0, idx_bits, idx_step, jnp.zeros((1, t), jnp.int32))

    qt = qt_ref[0, 0]
    m_ref[...] = jnp.full(m_ref.shape, NEG_BIG, jnp.float32)
    acc_ref[...] = jnp.zeros(acc_ref.shape, jnp.float32)
    jthr = jthr_ref[...]

    def attn_tile(kt, carry):
        kk = key_ref[kt]
        gk = krow + kt * t
        sel = jnp.logical_or(kk > thr, jnp.logical_and(kk == thr, gk <= jthr))
        sel = jnp.logical_and(sel, gk <= qcol + qb * t)
        bias = jnp.where(sel, 0.0, NEG_BIG)
        k_t = k_ref[0, pl.ds(kt * t, t), :]
        vt_t = vt_ref[0, kt]
        s_all = jnp.dot(k_t, qt, preferred_element_type=jnp.float32)
        for h in range(nh):
            s = s_all[:, h * t:(h + 1) * t] + bias
            m_old = m_ref[h:h + 1, :]
            m_new = jnp.maximum(m_old, jnp.max(s, axis=0, keepdims=True))
            alpha = jnp.exp2(m_old - m_new)
            p = jnp.exp2(s - m_new)
            acc_ref[h] = acc_ref[h] * alpha + jnp.dot(
                vt_t, p.astype(jnp.bfloat16), preferred_element_type=jnp.float32)
            m_ref[h:h + 1, :] = m_new
        return carry

    lax.fori_loop(0, n_kt, attn_tile, 0)
    dh = o_ref.shape[2] // nh
    for h in range(nh):
        acc = acc_ref[h]
        out = acc[:dh] / acc[dh:dh + 1]
        o_ref[0, :, h * dh:(h + 1) * dh] = out.T.astype(o_ref.dtype)


def _attention(q, qi, wi, k, v, ki, topk):
    b, s, nh, dh = q.shape
    di = qi.shape[3]
    t = min(ATTN_TILE, s)
    nq = s // t
    bf = jnp.bfloat16
    q = q * (float(dh) ** -0.5 * LOG2E)
    qt = q.astype(bf).reshape(b, nq, t, nh, dh).transpose(0, 1, 4, 3, 2).reshape(b, nq, dh, nh * t)
    qit = qi.astype(bf).reshape(b, nq, t, nh, di).transpose(0, 1, 4, 3, 2).reshape(b, nq, di, nh * t)
    wit = wi.reshape(b, nq, t, nh).transpose(0, 1, 3, 2).reshape(b, nq, 1, nh * t)
    vt = v.astype(bf).reshape(b, nq, t, dh).swapaxes(2, 3)
    vt = jnp.concatenate([vt, jnp.ones((b, nq, SUBLANES, t), bf)], axis=2)
    dhe = dh + SUBLANES
    kern = functools.partial(_attn_kernel, t=t, nh=nh, topk=topk)
    return pl.pallas_call(
        kern,
        grid=(b, nq),
        in_specs=[
            pl.BlockSpec((1, 1, dh, nh * t), lambda bi, qb: (bi, qb, 0, 0)),
            pl.BlockSpec((1, 1, di, nh * t), lambda bi, qb: (bi, qb, 0, 0)),
            pl.BlockSpec((1, 1, 1, nh * t), lambda bi, qb: (bi, qb, 0, 0)),
            pl.BlockSpec((1, s, dh), lambda bi, qb: (bi, 0, 0)),
            pl.BlockSpec((1, nq, dhe, t), lambda bi, qb: (bi, 0, 0, 0)),
            pl.BlockSpec((1, s, di), lambda bi, qb: (bi, 0, 0)),
        ],
        out_specs=pl.BlockSpec((1, t, nh * dh), lambda bi, qb: (bi, qb, 0)),
        out_shape=jax.ShapeDtypeStruct((b, s, nh * dh), jnp.bfloat16),
        scratch_shapes=[
            pltpu.VMEM((nq, t, t), jnp.int32),
            pltpu.VMEM((1, t), jnp.int32),
            pltpu.VMEM((nh, t), jnp.float32),
            pltpu.VMEM((nh, dhe, t), jnp.float32),
        ],
        compiler_params=pltpu.CompilerParams(
            dimension_semantics=("parallel", "arbitrary"),
            vmem_limit_bytes=VMEM_LIMIT_BYTES),
        name="dsa_attention",
    )(qt, qit, wit, k.astype(bf), vt, ki.astype(bf))


def _bdot(a, b, dims=(((1,), (0,)), ((), ()))):
    return lax.dot_general(a.astype(jnp.bfloat16), b.astype(jnp.bfloat16), dims,
                           preferred_element_type=jnp.float32)


_NT = (((1,), (1,)), ((), ()))
_TN = (((0,), (0,)), ((), ()))


def _scan_kernel(r_ref, lw_ref, k_ref, v_ref, kk_ref, b_ref, y_ref, s_ref, *, chunk):
    @pl.when(pl.program_id(1) == 0)
    def _():
        s_ref[...] = jnp.zeros(s_ref.shape, jnp.float32)

    n_tiles = s_ref.shape[0]
    hl = LANES // 2
    gt = SCAN_GROUP_TILES
    rows = 2 * gt * chunk
    row = lax.broadcasted_iota(jnp.int32, (rows, rows), 0)
    col = lax.broadcasted_iota(jnp.int32, (rows, rows), 1)
    same = (row // chunk) == (col // chunk)
    incl = jnp.logical_and(same, row >= col)
    strict = jnp.logical_and(same, row > col)
    r1 = lax.broadcasted_iota(jnp.int32, (chunk, chunk), 0)
    c1 = lax.broadcasted_iota(jnp.int32, (chunk, chunk), 1)
    tril = jnp.where(r1 >= c1, 1.0, 0.0).astype(jnp.float32)
    lane = lax.broadcasted_iota(jnp.int32, (1, LANES), 1)
    even = lane < hl
    pr_ = lax.broadcasted_iota(jnp.int32, (LANES, LANES), 0)
    pc_ = lax.broadcasted_iota(jnp.int32, (LANES, LANES), 1)
    pair_bd = (pr_ // hl) == (pc_ // hl)

    lw = lw_ref[0]
    c = jnp.dot(tril, lw, precision=lax.Precision.HIGHEST,
                preferred_element_type=jnp.float32)
    e_neg = jnp.exp(-c)
    at = -kk_ref[0] * jnp.exp(c - lw)
    rt = r_ref[0] * jnp.exp(c)
    bt = b_ref[0] * e_neg
    kt = k_ref[0] * e_neg
    v = v_ref[0]
    p_last = jnp.exp(c[chunk - 1:chunk, :])
    bp = bt * p_last
    kp = kt * p_last

    def tile(x, t):
        return x[:, t * LANES:(t + 1) * LANES]

    def stack_masked(x, t0):
        parts = []
        for t in range(t0, t0 + gt):
            xt = tile(x, t)
            parts += [jnp.where(even, xt, 0.0), jnp.where(even, 0.0, xt)]
        return jnp.concatenate(parts, axis=0)

    def stack_plain(x, t0):
        return jnp.concatenate([tile(x, t) for t in range(t0, t0 + gt) for _ in range(2)], axis=0)

    steps = max(1, int(np.ceil(np.log2(chunk))))
    groups = list(range(0, n_tiles, gt))
    at_s = [stack_masked(at, t0) for t0 in groups]
    v_s = [stack_masked(v, t0) for t0 in groups]
    a_all = [_bdot(jnp.concatenate([at_s[g], stack_masked(rt, t0)], axis=0),
                   jnp.concatenate([stack_plain(bt, t0), stack_plain(kt, t0)], axis=0), _NT)
             for g, t0 in enumerate(groups)]
    mpow = [jnp.where(strict, a[:rows, :rows], 0.0) for a in a_all]
    a_ak = [jnp.where(strict, a[:rows, rows:], 0.0) for a in a_all]
    a_rb = [jnp.where(incl, a[rows:, :rows], 0.0) for a in a_all]
    a_rk = [jnp.where(incl, a[rows:, rows:], 0.0) for a in a_all]

    xs = [jnp.concatenate([at_s[g], _bdot(a_ak[g], v_s[g])], axis=1) for g in range(len(groups))]
    for i in range(steps):
        xs = [x + _bdot(m_, x) for x, m_ in zip(xs, mpow)]
        if i + 1 < steps:
            mpow = [_bdot(m_, m_) for m_ in mpow]
    qys = [_bdot(a_rb[g], xs[g]) for g in range(len(groups))]
    y0s = [qys[g][:, LANES:] + _bdot(a_rk[g], v_s[g]) for g in range(len(groups))]

    for g, t0 in enumerate(groups):
        x = xs[g]
        y0_s = y0s[g]
        qh_s = qys[g][:, :LANES]
        for j in range(gt):
            t = t0 + j
            lo, mid, hi = 2 * j * chunk, (2 * j + 1) * chunk, (2 * j + 2) * chunk
            unstack = lambda z: z[lo:mid] + z[mid:hi]
            x_t = unstack(x)
            qh_t = tile(rt, t) + unstack(qh_s)
            y0_t = unstack(y0_s)
            s0 = s_ref[t]
            gw = _bdot(x_t, tile(bp, t), _TN)
            g_t = jnp.where(pair_bd, gw[:LANES], 0.0)
            h_t = jnp.where(pair_bd, gw[LANES:] + _bdot(tile(v, t), tile(kp, t), _TN), 0.0)
            y_ref[0, :, t * LANES:(t + 1) * LANES] = y0_t + _bdot(qh_t, s0, _NT)
            s_ref[t] = s0 * tile(p_last, t) + _bdot(s0, g_t) + h_t


def _rwkv_scan(r, lw, k, v, kk, b, r_col=0, v_col=0):
    bsz, s, _ = lw.shape
    d = D_R
    chunk = min(SCAN_CHUNK, s)
    spec = pl.BlockSpec((1, chunk, d), lambda bi, ci: (bi, ci, 0))
    colspec = lambda j: pl.BlockSpec((1, chunk, d), lambda bi, ci, j=j: (bi, ci, j))
    kern = functools.partial(_scan_kernel, chunk=chunk)
    return pl.pallas_call(
        kern,
        grid=(bsz, s // chunk),
        in_specs=[colspec(r_col), spec, spec, colspec(v_col), spec, spec],
        out_specs=spec,
        out_shape=jax.ShapeDtypeStruct((bsz, s, d), jnp.float32),
        scratch_shapes=[pltpu.VMEM((d // LANES, LANES, LANES), jnp.float32)],
        compiler_params=pltpu.CompilerParams(
            dimension_semantics=("parallel", "arbitrary"),
            vmem_limit_bytes=VMEM_LIMIT_BYTES),
        name="rwkv7_scan",
    )(r, lw, k, v, kk, b)


RW_COL_LORA = 3 * D_R // LANES
RW_COL_GATE = RW_COL_LORA + 1
RW_COL_VRES = RW_COL_LORA + 2


def _rwprep_kernel(*refs, has_vres):
    if has_vres:
        (pk_ref, pv_ref, pwa_ref, pg_ref, pvr_ref, vf_ref, w0_ref, a0_ref, v0_ref, kk_ref, ka_ref,
         w2_ref, a2_ref, g2_ref, v2_ref, seg_ref, lw_ref, k_out, kkn_ref, b_ref, g_out, v_out) = refs
    else:
        (pk_ref, pwa_ref, pg_ref, w0_ref, a0_ref, kk_ref, ka_ref,
         w2_ref, a2_ref, g2_ref, seg_ref, lw_ref, k_out, kkn_ref, b_ref, g_out) = refs
    bf = jnp.bfloat16
    dotf = lambda a, w: jnp.dot(a.astype(bf), w, preferred_element_type=jnp.float32)
    pwa = pwa_ref[...]
    u = w0_ref[...] + dotf(jnp.tanh(pwa), w2_ref[...])
    neg = -u
    softplus = jnp.maximum(neg, 0.0) + jnp.log(1.0 + jnp.exp(-jnp.abs(neg)))
    lw_ref[...] = -jnp.exp(-softplus - 0.5)
    a = jax.nn.sigmoid(a0_ref[...] + dotf(pwa, a2_ref[...]))
    g_out[...] = dotf(jax.nn.sigmoid(pg_ref[...]), g2_ref[...])
    pk = pk_ref[...]
    kk = pk * kk_ref[...]
    norm = jnp.sqrt(_head_sum(kk * kk, seg_ref[...]))
    kk = kk / jnp.maximum(norm, 1e-12)
    kkn_ref[...] = kk
    b_ref[...] = kk * a
    k_out[...] = pk * (1.0 + (a - 1.0) * ka_ref[...])
    if has_vres:
        pv = pv_ref[...]
        mix = jax.nn.sigmoid(v0_ref[...] + dotf(pvr_ref[...], v2_ref[...]))
        v_out[...] = pv + (vf_ref[...] - pv) * mix


def _rw_prep(p_rw, v_first, w0, a0, v0, k_k, k_a, w2, a2, g2, v2):
    m = p_rw.shape[0]
    d = D_R
    tm = min(256, m)
    has_vres = v_first is not None
    bf = jnp.bfloat16
    hl = LANES // 2
    seg = (jnp.arange(LANES)[:, None] // hl == jnp.arange(LANES)[None, :] // hl).astype(bf)
    pad_rows = lambda w, top: jnp.concatenate(
        [jnp.zeros((top, d), w.dtype), w, jnp.zeros((LANES - top - w.shape[0], d), w.dtype)], axis=0).astype(bf)
    col = lambda j, width: pl.BlockSpec((tm, width), lambda i, j=j: (i, j))
    vec = pl.BlockSpec((1, d), lambda i: (0, 0))
    wspec = pl.BlockSpec((LANES, d), lambda i: (0, 0))
    segspec = pl.BlockSpec((LANES, LANES), lambda i: (0, 0))
    row_out = pl.BlockSpec((tm, d), lambda i: (i, 0))
    v1 = lambda t_: t_.reshape(1, d)
    if has_vres:
        args = [p_rw, p_rw, p_rw, p_rw, p_rw, v_first, v1(w0), v1(a0), v1(v0), v1(k_k), v1(k_a),
                pad_rows(w2, 0), pad_rows(a2, LORA_W), g2.astype(bf), pad_rows(v2, 0), seg]
        in_specs = [col(1, d), col(2, d), col(RW_COL_LORA, LANES), col(RW_COL_GATE, LANES),
                    col(RW_COL_VRES, LANES), col(2, d), vec, vec, vec, vec, vec,
                    wspec, wspec, wspec, wspec, segspec]
        n_out = 6
    else:
        args = [p_rw, p_rw, p_rw, v1(w0), v1(a0), v1(k_k), v1(k_a),
                pad_rows(w2, 0), pad_rows(a2, LORA_W), g2.astype(bf), seg]
        in_specs = [col(1, d), col(RW_COL_LORA, LANES), col(RW_COL_GATE, LANES), vec, vec, vec, vec,
                    wspec, wspec, wspec, segspec]
        n_out = 5
    return pl.pallas_call(
        functools.partial(_rwprep_kernel, has_vres=has_vres),
        grid=(m // tm,),
        in_specs=in_specs,
        out_specs=[row_out] * n_out,
        out_shape=[jax.ShapeDtypeStruct((m, d), jnp.float32)] * n_out,
        compiler_params=pltpu.CompilerParams(
            dimension_semantics=("parallel",),
            vmem_limit_bytes=VMEM_LIMIT_BYTES),
        name="rw_prep",
    )(*args)


def _head_sum(z, seg):
    hi = z.astype(jnp.bfloat16)
    lo = (z - hi.astype(jnp.float32)).astype(jnp.bfloat16)
    outs = []
    for t in range(z.shape[1] // LANES):
        sl = slice(t * LANES, (t + 1) * LANES)
        outs.append(jnp.dot(hi[:, sl], seg, preferred_element_type=jnp.float32)
                    + jnp.dot(lo[:, sl], seg, preferred_element_type=jnp.float32))
    return jnp.concatenate(outs, axis=1)


def _post_kernel(y_ref, r_ref, k_ref, v_ref, g_ref, o_ref, gate_ref, x_ref,
                 glnx_ref, blnx_ref, rk_ref, lng_ref, lnb_ref, seg_ref, wob_ref, woa_ref, wout_ref,
                 out_ref, *, alpha):
    seg = seg_ref[...]
    d_model = x_ref.shape[1]
    inv_n = 1.0 / N_R
    y = y_ref[...]
    dy = y - _head_sum(y, seg) * inv_n
    var = _head_sum(dy * dy, seg) * inv_n
    yn = dy * lax.rsqrt(var + GN_EPS) * glnx_ref[...] + blnx_ref[...]
    bonus = _head_sum(r_ref[...] * k_ref[...] * rk_ref[...], seg) * v_ref[...]
    yb_in = ((yn + bonus) * g_ref[...]).astype(jnp.bfloat16)
    y_b = jnp.dot(yb_in, wob_ref[...], preferred_element_type=jnp.float32)
    y_a = jnp.dot(o_ref[...], woa_ref[...], preferred_element_type=jnp.float32)
    gates = gate_ref[...]
    z = gates[:, :d_model] * y_a + gates[:, d_model:] * y_b
    t = alpha * x_ref[...] + jnp.dot(z.astype(jnp.bfloat16), wout_ref[...],
                                     preferred_element_type=jnp.float32)
    mu = jnp.mean(t, axis=-1, keepdims=True)
    dt = t - mu
    vt = jnp.mean(dt * dt, axis=-1, keepdims=True)
    out_ref[...] = dt * lax.rsqrt(vt + LN_EPS) * lng_ref[...] + lnb_ref[...]


def _mixer_post(y, r, k, v, g, o, gates, x, g_lnx, b_lnx, r_k, ln_g, ln_b, w_ob, w_oa, w_out, alpha, v_col=0):
    m, d = x.shape
    tm = min(256, m)
    hl = LANES // 2
    seg = (jnp.arange(LANES)[:, None] // hl == jnp.arange(LANES)[None, :] // hl).astype(jnp.bfloat16)
    row = lambda n, j=0: pl.BlockSpec((tm, n), lambda i: (i, j))
    vec = pl.BlockSpec((1, d), lambda i: (0, 0))
    full = lambda a, b_: pl.BlockSpec((a, b_), lambda i: (0, 0))
    bf = jnp.bfloat16
    return pl.pallas_call(
        functools.partial(_post_kernel, alpha=alpha),
        grid=(m // tm,),
        in_specs=[row(d), row(d), row(d), row(d, v_col), row(d), row(d), row(2 * d), row(d),
                  vec, vec, vec, vec, vec, full(LANES, LANES), full(d, d), full(d, d), full(d, d)],
        out_specs=row(d),
        out_shape=jax.ShapeDtypeStruct((m, d), jnp.float32),
        compiler_params=pltpu.CompilerParams(
            dimension_semantics=("parallel",),
            vmem_limit_bytes=VMEM_LIMIT_BYTES),
        name="mixer_post",
    )(y, r, k, v, g, o, gates, x,
      g_lnx.reshape(1, d), b_lnx.reshape(1, d), r_k.reshape(1, d), ln_g.reshape(1, d), ln_b.reshape(1, d),
      seg, w_ob.astype(bf), w_oa.astype(bf), w_out.astype(bf))


def _layer_norm(x, g, b, eps=LN_EPS):
    mu = jnp.mean(x, -1, keepdims=True)
    var = jnp.mean(jnp.square(x - mu), -1, keepdims=True)
    return (x - mu) * lax.rsqrt(var + eps) * g + b


def _rms_norm(x, g, eps=1e-6):
    return x * lax.rsqrt(jnp.mean(jnp.square(x), -1, keepdims=True) + eps) * g


def _rope_tables(positions, rot_dim):
    inv_freq = jnp.power(jnp.float32(ROPE_THETA), -jnp.arange(0, rot_dim, 2, dtype=jnp.float32) / rot_dim)
    ang = positions.astype(jnp.float32)[..., None] * inv_freq
    return jnp.cos(ang)[:, :, None, :], jnp.sin(ang)[:, :, None, :]


def _partial_rope(x, cos, sin):
    half = cos.shape[-1]
    x1, x2, rest = x[..., :half], x[..., half:2 * half], x[..., 2 * half:]
    return jnp.concatenate([x1 * cos - x2 * sin, x2 * cos + x1 * sin, rest], -1)


def kernel(x, positions, w_in0, w_in_rest, b_gate, g_cq, w_uq, w_iq, g_ik, b_ik, w_oa, mu_rwkv, mu_vres, w0, w2, a0, a2, v0, v2, g2, k_k, k_a, r_k, g_lnx, b_lnx, w_ob, w_out, ln1_g, ln1_b, w_up, conv_w, conv_b, w_down, ln2_g, ln2_b):
    bsz, seq, d = x.shape
    m = bsz * seq
    alpha = (2 * DEPTH) ** 0.25
    topk = min(TOPK_MAX, seq // 4)
    rope_a = _rope_tables(positions, ROT_A)
    rope_i = _rope_tables(positions, ROT_IDX)
    v_first = None
    mm3 = lambda t, w: _mm(t.reshape(m, t.shape[-1]), w).reshape(bsz, seq, w.shape[1])

    for i in range(DEPTH):
        w_in = w_in0 if i == 0 else w_in_rest[i - 1]
        zcols = lambda n: jnp.zeros((d, n), w_in.dtype)
        w_att = jnp.concatenate([w_in[:, :N_ATT], zcols(N_ATT_PAD - N_ATT)], axis=1)
        n_rw = w_in.shape[1] - (N_ATT + N_GATE)
        w_rw = jnp.concatenate([w_in[:, N_ATT + N_GATE:], zcols(N_RW_PAD - n_rw)], axis=1)
        mu_parts = [mu_rwkv[i]] + ([mu_vres[i - 1]] if i > 0 else [])
        mu_rw = jnp.concatenate(mu_parts + [jnp.zeros((N_RW_PAD - n_rw,), jnp.float32)])
        p_att = mm3(x, w_att)
        gates = _mm_gate(x.reshape(m, d), w_in[:, N_ATT:N_ATT + N_GATE], b_gate[i]).reshape(bsz, seq, N_GATE)
        p_rw = _mm_shift(x, w_rw, mu_rw)

        cq_raw = p_att[..., :D_CQ]
        k_raw = p_att[..., D_CQ:D_CQ + DH_A]
        v_raw = p_att[..., D_CQ + DH_A:D_CQ + 2 * DH_A]
        ik_raw = p_att[..., D_CQ + 2 * DH_A:D_CQ + 2 * DH_A + D_IDX]
        iw_raw = p_att[..., D_CQ + 2 * DH_A + D_IDX:N_ATT]
        c_q = _rms_norm(cq_raw, g_cq[i])
        q = _partial_rope(mm3(c_q, w_uq[i]).reshape(bsz, seq, H_A, DH_A), *rope_a)
        k_att = _partial_rope(k_raw[:, :, None, :], *rope_a)[:, :, 0]
        qi = _partial_rope(mm3(c_q, w_iq[i]).reshape(bsz, seq, H_IDX, D_IDX), *rope_i)
        ki = _partial_rope(_layer_norm(ik_raw, g_ik[i], b_ik[i])[:, :, None, :], *rope_i)[:, :, 0]
        wi = iw_raw * (H_IDX ** -0.5 * D_IDX ** -0.5)
        o = _attention(q, qi, wi, k_att, v_raw, ki, topk)

        f2 = lambda t_: t_.reshape(m, t_.shape[-1])
        f3 = lambda t_: t_.reshape(bsz, seq, t_.shape[-1])
        v_col = 2
        if i == 0:
            v_first = p_rw
            lw, k_r, kk_n, b_r, g_r = _rw_prep(f2(p_rw), None, w0[i], a0[i], None, k_k[i], k_a[i],
                                               w2[i], a2[i], g2[i], None)
            v_r = p_rw
        else:
            lw, k_r, kk_n, b_r, g_r, v_r = _rw_prep(f2(p_rw), f2(v_first), w0[i], a0[i], v0[i - 1],
                                                    k_k[i], k_a[i], w2[i], a2[i], g2[i], v2[i - 1])
            v_r = f3(v_r)
            v_col = 0
        y = _rwkv_scan(p_rw, f3(lw), f3(k_r), v_r, f3(kk_n), f3(b_r), r_col=0, v_col=v_col)

        x = _mixer_post(f2(y), f2(p_rw), k_r, f2(v_r), g_r, f2(o), f2(gates), f2(x),
                        g_lnx[i], b_lnx[i], r_k[i], ln1_g[i], ln1_b[i],
                        w_ob[i], w_oa[i], w_out[i], alpha, v_col=v_col).reshape(bsz, seq, d)

        x = _ffn(x, w_up[i], conv_w[i], conv_b[i], w_down[i], ln2_g[i], ln2_b[i], alpha)
    return x
```

```python
import functools

import jax
import jax.numpy as jnp
import numpy as np
from jax import lax
from jax.experimental import pallas as pl
from jax.experimental.pallas import tpu as pltpu

D_MODEL = 1024
DEPTH = 4
H_A = 8
DH_A = 128
D_CQ = 256
ROT_A = DH_A // 4
H_IDX = 8
D_IDX = 64
ROT_IDX = D_IDX // 4
TOPK_MAX = 256
ROPE_THETA = 500000.0
N_R = 64
H_R = D_MODEL // N_R
D_R = H_R * N_R
LORA_W = 64
LORA_A = 64
LORA_V = 32
LORA_G = 128
GN_EPS = 64e-5
D_FF = ((8 * D_MODEL // 3 + 127) // 128) * 128
LN_EPS = 1e-5

N_ATT = D_CQ + 2 * DH_A + D_IDX + H_IDX
N_GATE = 2 * D_MODEL
N_RWKV = 3 * D_R + LORA_W + LORA_A + LORA_G
N_IN0 = N_ATT + N_GATE + N_RWKV

LANES = 128
SUBLANES = 8
LOG2E = 1.4426950408889634
VMEM_LIMIT_BYTES = 48 * 1024 * 1024

N_ATT_PAD = 640
N_RW_PAD = 3584

ATTN_TILE = 256
SCAN_CHUNK = 64
SCAN_GROUP_TILES = 1
NEG_BIG = -1e30
INT_MIN = -(2 ** 31)


def _pick_tile(n, cap):
    best = None
    for t in range(LANES, min(n, cap) + 1, LANES):
        if n % t == 0:
            best = t
    return best if best is not None else n


def _gate_kernel(x_ref, w_ref, b_ref, o_ref):
    p = jnp.dot(x_ref[...].astype(jnp.bfloat16), w_ref[...], preferred_element_type=jnp.float32)
    o_ref[...] = jax.nn.sigmoid(p + b_ref[...])


def _mm_gate(x, w, bias):
    m, k = x.shape
    n = w.shape[1]
    tm = min(1024, m)
    tn = _pick_tile(n, 1024)
    return pl.pallas_call(
        _gate_kernel,
        grid=(m // tm, n // tn),
        in_specs=[pl.BlockSpec((tm, k), lambda i, j: (i, 0)),
                  pl.BlockSpec((k, tn), lambda i, j: (0, j)),
                  pl.BlockSpec((1, tn), lambda i, j: (0, j))],
        out_specs=pl.BlockSpec((tm, tn), lambda i, j: (i, j)),
        out_shape=jax.ShapeDtypeStruct((m, n), jnp.float32),
        compiler_params=pltpu.CompilerParams(
            dimension_semantics=("parallel", "arbitrary"),
            vmem_limit_bytes=VMEM_LIMIT_BYTES),
        name="mm_gate",
    )(x, w.astype(jnp.bfloat16), bias.reshape(1, n))


def _shift_rows(cur, prev_rows, k):
    rolled = pltpu.roll(cur, k, 0)
    row = lax.broadcasted_iota(jnp.int32, cur.shape, 0)
    for i in range(k):
        rolled = jnp.where(row == i, prev_rows[i:i + 1, :], rolled)
    return rolled


def _shift_mm_kernel(x_ref, w_ref, mu_ref, o_ref, carry_ref):
    si = pl.program_id(1)
    j = pl.program_id(2)
    tm = o_ref.shape[1]
    p = jnp.dot(x_ref[0].astype(jnp.bfloat16), w_ref[...], preferred_element_type=jnp.float32)
    @pl.when(si == 0)
    def _():
        carry_ref[j] = jnp.zeros(carry_ref.shape[1:], jnp.float32)

    prev = _shift_rows(p, carry_ref[j, 0:1, :], 1)
    carry_ref[j, 0:1, :] = p[tm - 1:tm, :]
    o_ref[0] = p + (prev - p) * mu_ref[...]


def _mm_shift(x, w, mu):
    b, s, k = x.shape
    n = w.shape[1]
    tm = min(1024, s)
    tn = 512 if n % 512 == 0 else _pick_tile(n, 1024)
    return pl.pallas_call(
        _shift_mm_kernel,
        grid=(b, s // tm, n // tn),
        in_specs=[pl.BlockSpec((1, tm, k), lambda bi, si, j: (bi, si, 0)),
                  pl.BlockSpec((k, tn), lambda bi, si, j: (0, j)),
                  pl.BlockSpec((1, tn), lambda bi, si, j: (0, j))],
        out_specs=pl.BlockSpec((1, tm, tn), lambda bi, si, j: (bi, si, j)),
        out_shape=jax.ShapeDtypeStruct((b, s, n), jnp.float32),
        scratch_shapes=[pltpu.VMEM((n // tn, 8, tn), jnp.float32)],
        compiler_params=pltpu.CompilerParams(
            dimension_semantics=("parallel", "arbitrary", "arbitrary"),
            vmem_limit_bytes=VMEM_LIMIT_BYTES),
        name="mm_shift",
    )(x, w.astype(jnp.bfloat16), mu.reshape(1, n))


def _ffn_kernel(x_ref, wg_ref, wv_ref, cw_ref, cb_ref, wd_ref, g_ref, b_ref, o_ref,
                xb_ref, acc_ref, carry_ref, *, alpha, eps):
    si = pl.program_id(1)
    c = pl.program_id(2)
    tm = o_ref.shape[1]

    @pl.when(c == 0)
    def _():
        xb_ref[...] = x_ref[0].astype(jnp.bfloat16)
        acc_ref[...] = jnp.zeros(acc_ref.shape, jnp.float32)

    xb = xb_ref[...]
    hg = jnp.dot(xb, wg_ref[...], preferred_element_type=jnp.float32)
    hv = jnp.dot(xb, wv_ref[...], preferred_element_type=jnp.float32)
    @pl.when(si == 0)
    def _():
        carry_ref[c] = jnp.zeros(carry_ref.shape[1:], jnp.float32)

    tail = carry_ref[c, 0:2, :]
    g1 = _shift_rows(hg, tail[1:2, :], 1)
    g2 = _shift_rows(hg, tail, 2)
    carry_ref[c, 0:2, :] = hg[tm - 2:tm, :]
    cw = cw_ref[...]
    conv = g2 * cw[0:1, :] + g1 * cw[1:2, :] + hg * cw[2:3, :] + cb_ref[...]
    act = jax.nn.silu(conv) * hv
    acc_ref[...] += jnp.dot(act.astype(jnp.bfloat16), wd_ref[...], preferred_element_type=jnp.float32)

    @pl.when(c == pl.num_programs(2) - 1)
    def _():
        y = alpha * x_ref[0] + acc_ref[...]
        mu = jnp.mean(y, axis=-1, keepdims=True)
        d = y - mu
        var = jnp.mean(d * d, axis=-1, keepdims=True)
        o_ref[0] = d * lax.rsqrt(var + eps) * g_ref[...] + b_ref[...]


def _ffn(x, w_up, conv_w, conv_b, w_down, ln_g, ln_b, alpha):
    b, s, d = x.shape
    dff = w_down.shape[0]
    tm = min(1024, s)
    tf = 256 if dff % 256 == 0 else LANES
    nc = dff // tf
    kern = functools.partial(_ffn_kernel, alpha=alpha, eps=LN_EPS)
    w_up = w_up.astype(jnp.bfloat16)
    return pl.pallas_call(
        kern,
        grid=(b, s // tm, nc),
        in_specs=[
            pl.BlockSpec((1, tm, d), lambda bi, si, c: (bi, si, 0)),
            pl.BlockSpec((d, tf), lambda bi, si, c: (0, c)),
            pl.BlockSpec((d, tf), lambda bi, si, c: (0, c + nc)),
            pl.BlockSpec((3, tf), lambda bi, si, c: (0, c)),
            pl.BlockSpec((1, tf), lambda bi, si, c: (0, c)),
            pl.BlockSpec((tf, d), lambda bi, si, c: (c, 0)),
            pl.BlockSpec((1, d), lambda bi, si, c: (0, 0)),
            pl.BlockSpec((1, d), lambda bi, si, c: (0, 0)),
        ],
        out_specs=pl.BlockSpec((1, tm, d), lambda bi, si, c: (bi, si, 0)),
        out_shape=jax.ShapeDtypeStruct((b, s, d), jnp.float32),
        scratch_shapes=[
            pltpu.VMEM((tm, d), jnp.bfloat16),
            pltpu.VMEM((tm, d), jnp.float32),
            pltpu.VMEM((nc, 8, tf), jnp.float32),
        ],
        compiler_params=pltpu.CompilerParams(
            dimension_semantics=("parallel", "arbitrary", "arbitrary"),
            vmem_limit_bytes=VMEM_LIMIT_BYTES),
        name="ffn",
    )(x, w_up, w_up, conv_w, conv_b.reshape(1, dff), w_down.astype(jnp.bfloat16),
      ln_g.reshape(1, d), ln_b.reshape(1, d))


def _sortable_key(score):
    score = jnp.where(score == 0.0, 0.0, score)
    bits = pltpu.bitcast(score, jnp.int32)
    return jnp.where(bits < 0, bits ^ jnp.int32(0x7FFFFFFF), bits)


def _attn_kernel(qt_ref, qit_ref, wi_ref, k_ref, vt_ref, ki_ref, o_ref,
                 key_ref, jthr_ref, m_ref, acc_ref, *, t, nh, topk):
    qb = pl.program_id(1)
    n_kt = qb + 1
    krow = lax.broadcasted_iota(jnp.int32, (t, t), 0)
    qcol = lax.broadcasted_iota(jnp.int32, (t, t), 1)

    qit = qit_ref[0, 0]
    wi = wi_ref[0, 0]

    def score_tile(kt, carry):
        ki_t = ki_ref[0, pl.ds(kt * t, t), :]
        rel = jnp.dot(ki_t, qit, preferred_element_type=jnp.float32)
        rel = jnp.maximum(rel, 0.0) * wi
        score = rel[:, 0:t]
        for h in range(1, nh):
            score = score + rel[:, h * t:(h + 1) * t]
        score = jnp.where(jnp.logical_and(kt == qb, krow > qcol), -jnp.inf, score)
        key_ref[kt] = _sortable_key(score)
        return carry

    lax.fori_loop(0, n_kt, score_tile, 0)

    def count(pred_fn):
        def body(kt, acc):
            c = jnp.where(pred_fn(key_ref[kt], kt), 1, 0).astype(jnp.int32)
            return acc + jnp.sum(c.reshape(t // 8, 8, t), axis=0)
        acc = lax.fori_loop(0, n_kt, body, jnp.zeros((8, t), jnp.int32))
        return jnp.sum(acc, axis=0, keepdims=True)

    def count_ge(cand):
        return count(lambda kk, kt: kk >= cand)

    thr0 = jnp.where(count_ge(jnp.zeros((1, t), jnp.int32)) >= topk,
                     jnp.int32(0), jnp.int32(INT_MIN))

    def bit_step(i, thr):
        cand = thr | jnp.left_shift(jnp.int32(1), 30 - i)
        return jnp.where(count_ge(cand) >= topk, cand, thr)

    thr = lax.fori_loop(0, 31, bit_step, thr0)
    jthr_ref[...] = jnp.full(jthr_ref.shape, 2 ** 30, jnp.int32)
    n_ge = count_ge(thr)

    @pl.when(jnp.max(n_ge) > topk)
    def _():
        need = topk - count(lambda kk, kt: kk > thr)
        idx_bits = int(np.ceil(np.log2(key_ref.shape[0] * t)))

        def idx_step(i, j):
            cand = j | jnp.left_shift(jnp.int32(1), idx_bits - 1 - i)
            below = count(lambda kk, kt: jnp.logical_and(kk == thr, krow + kt * t < cand))
            return jnp.where(below < need, cand, j)

        jthr_ref[...] = lax.fori_loop(0, idx_bits, idx_step, jnp.zeros((1, t), jnp.int32))

    qt = qt_ref[0, 0]
    m_ref[...] = jnp.full(m_ref.shape, NEG_BIG, jnp.float32)
    acc_ref[...] = jnp.zeros(acc_ref.shape, jnp.float32)
    jthr = jthr_ref[...]

    def attn_tile(kt, carry):
        kk = key_ref[kt]
        gk = krow + kt * t
        sel = jnp.logical_or(kk > thr, jnp.logical_and(kk == thr, gk <= jthr))
        sel = jnp.logical_and(sel, gk <= qcol + qb * t)
        bias = jnp.where(sel, 0.0, NEG_BIG)
        k_t = k_ref[0, pl.ds(kt * t, t), :]
        vt_t = vt_ref[0, kt]
        s_all = jnp.dot(k_t, qt, preferred_element_type=jnp.float32)
        for h in range(nh):
            s = s_all[:, h * t:(h + 1) * t] + bias
            m_old = m_ref[h:h + 1, :]
            m_new = jnp.maximum(m_old, jnp.max(s, axis=0, keepdims=True))
            alpha = jnp.exp2(m_old - m_new)
            p = jnp.exp2(s - m_new)
            acc_ref[h] = acc_ref[h] * alpha + jnp.dot(
                vt_t, p.astype(jnp.bfloat16), preferred_element_type=jnp.float32)
            m_ref[h:h + 1, :] = m_new
        return carry

    lax.fori_loop(0, n_kt, attn_tile, 0)
    dh = o_ref.shape[2] // nh
    for h in range(nh):
        acc = acc_ref[h]
        out = acc[:dh] / acc[dh:dh + 1]
        o_ref[0, :, h * dh:(h + 1) * dh] = out.T.astype(o_ref.dtype)


def _rope_rows(blk, cos, sin, half):
    x1, x2 = blk[:half], blk[half:2 * half]
    return jnp.concatenate([x1 * cos - x2 * sin, x2 * cos + x1 * sin, blk[2 * half:]], axis=0)


def _rope_lanes(x, c, sa, sb, half):
    return x * c + pltpu.roll(x, LANES - half, 1) * sa + pltpu.roll(x, half, 1) * sb


def _attprep_kernel(x_ref, watt_ref, gcq_ref, wuqt_ref, wiqt_ref, gik_ref, bik_ref,
                    cosa_ref, sina_ref, cosi_ref, sini_ref,
                    kc_ref, ksa_ref, ksb_ref, ic_ref, isa_ref, isb_ref,
                    qt_ref, qit_ref, wit_ref, k_ref, vt_ref, ki_ref, *, t, nh, q_scale, w_scale):
    bf = jnp.bfloat16
    nt = (((1,), (1,)), ((), ()))
    p = jnp.dot(x_ref[0].astype(bf), watt_ref[...], preferred_element_type=jnp.float32)
    cq = p[:, :D_CQ]
    c_q = cq * lax.rsqrt(jnp.mean(cq * cq, axis=-1, keepdims=True) + 1e-6) * gcq_ref[...]
    cqb = c_q.astype(bf)

    q_t = lax.dot_general(wuqt_ref[...], cqb, nt, preferred_element_type=jnp.float32)
    cos, sin = cosa_ref[0], sina_ref[0]
    for h in range(nh):
        blk = _rope_rows(q_t[h * DH_A:(h + 1) * DH_A], cos, sin, ROT_A // 2) * q_scale
        qt_ref[0, 0, :, h * t:(h + 1) * t] = blk.astype(bf)
    qi_t = lax.dot_general(wiqt_ref[...], cqb, nt, preferred_element_type=jnp.float32)
    cos, sin = cosi_ref[0], sini_ref[0]
    for h in range(nh):
        blk = _rope_rows(qi_t[h * D_IDX:(h + 1) * D_IDX], cos, sin, ROT_IDX // 2)
        qit_ref[0, 0, :, h * t:(h + 1) * t] = blk.astype(bf)

    k_raw = p[:, D_CQ:D_CQ + DH_A]
    k_ref[0] = _rope_lanes(k_raw, kc_ref[0], ksa_ref[0], ksb_ref[0], ROT_A // 2).astype(bf)
    v_raw = p[:, D_CQ + DH_A:D_CQ + 2 * DH_A]
    vt_ref[0, 0, :DH_A, :] = v_raw.T.astype(bf)
    vt_ref[0, 0, DH_A:, :] = jnp.ones((vt_ref.shape[2] - DH_A, t), bf)

    tail = p[:, D_CQ + 2 * DH_A:]
    lane = lax.broadcasted_iota(jnp.int32, (1, LANES), 1)
    is_key = lane < D_IDX
    mu = jnp.sum(jnp.where(is_key, tail, 0.0), axis=-1, keepdims=True) * (1.0 / D_IDX)
    dk = jnp.where(is_key, tail - mu, 0.0)
    var = jnp.sum(dk * dk, axis=-1, keepdims=True) * (1.0 / D_IDX)
    ln = dk * lax.rsqrt(var + LN_EPS) * gik_ref[...] + bik_ref[...]
    ki = _rope_lanes(ln, ic_ref[0], isa_ref[0], isb_ref[0], ROT_IDX // 2)
    ki_ref[0] = ki[:, :D_IDX].astype(bf)
    w_t = tail.T[D_IDX:D_IDX + nh] * w_scale
    for h in range(nh):
        wit_ref[0, 0, :, h * t:(h + 1) * t] = w_t[h:h + 1]


def _rope_tables_all(positions):
    def tables(rot):
        half = rot // 2
        inv_freq = jnp.power(jnp.float32(ROPE_THETA), -jnp.arange(0, rot, 2, dtype=jnp.float32) / rot)
        ang = positions.astype(jnp.float32)[..., None] * inv_freq
        cos, sin = jnp.cos(ang), jnp.sin(ang)
        ones = jnp.ones(ang.shape[:2] + (LANES - 2 * half,), jnp.float32)
        zeros = lambda n: jnp.zeros(ang.shape[:2] + (n,), jnp.float32)
        c = jnp.concatenate([cos, cos, ones], axis=-1)
        sa = jnp.concatenate([-sin, zeros(LANES - half)], axis=-1)
        sb = jnp.concatenate([zeros(half), sin, zeros(LANES - 2 * half)], axis=-1)
        return cos.swapaxes(1, 2), sin.swapaxes(1, 2), c, sa, sb
    return tables(ROT_A), tables(ROT_IDX)


def _att_prep(x, w_att, g_cq, w_uq, w_iq, g_ik, b_ik, tabs_a, tabs_i):
    b, s, d = x.shape
    t = min(ATTN_TILE, s)
    nq = s // t
    nh = H_A
    bf = jnp.bfloat16
    dhe = DH_A + SUBLANES
    pad_vec = lambda vv: jnp.concatenate([vv, jnp.zeros((LANES - vv.shape[0],), vv.dtype)]).reshape(1, LANES)
    cos_a, sin_a, kc, ksa, ksb = tabs_a
    cos_i, sin_i, ic, isa, isb = tabs_i
    const = lambda a, b_: pl.BlockSpec((a, b_), lambda bi, qb: (0, 0))
    fm = lambda rows: pl.BlockSpec((1, rows, t), lambda bi, qb: (bi, 0, qb))
    tm_ = pl.BlockSpec((1, t, LANES), lambda bi, qb: (bi, qb, 0))
    kern = functools.partial(_attprep_kernel, t=t, nh=nh,
                             q_scale=float(DH_A) ** -0.5 * LOG2E, w_scale=H_IDX ** -0.5 * D_IDX ** -0.5)
    return pl.pallas_call(
        kern,
        grid=(b, nq),
        in_specs=[
            pl.BlockSpec((1, t, d), lambda bi, qb: (bi, qb, 0)),
            const(d, N_ATT_PAD), const(1, D_CQ), const(nh * DH_A, D_CQ), const(nh * D_IDX, D_CQ),
            const(1, LANES), const(1, LANES),
            fm(ROT_A // 2), fm(ROT_A // 2), fm(ROT_IDX // 2), fm(ROT_IDX // 2),
            tm_, tm_, tm_, tm_, tm_, tm_,
        ],
        out_specs=[
            pl.BlockSpec((1, 1, DH_A, nh * t), lambda bi, qb: (bi, qb, 0, 0)),
            pl.BlockSpec((1, 1, D_IDX, nh * t), lambda bi, qb: (bi, qb, 0, 0)),
            pl.BlockSpec((1, 1, 1, nh * t), lambda bi, qb: (bi, qb, 0, 0)),
            pl.BlockSpec((1, t, DH_A), lambda bi, qb: (bi, qb, 0)),
            pl.BlockSpec((1, 1, dhe, t), lambda bi, qb: (bi, qb, 0, 0)),
            pl.BlockSpec((1, t, D_IDX), lambda bi, qb: (bi, qb, 0)),
        ],
        out_shape=[
            jax.ShapeDtypeStruct((b, nq, DH_A, nh * t), bf),
            jax.ShapeDtypeStruct((b, nq, D_IDX, nh * t), bf),
            jax.ShapeDtypeStruct((b, nq, 1, nh * t), jnp.float32),
            jax.ShapeDtypeStruct((b, s, DH_A), bf),
            jax.ShapeDtypeStruct((b, nq, dhe, t), bf),
            jax.ShapeDtypeStruct((b, s, D_IDX), bf),
        ],
        compiler_params=pltpu.CompilerParams(
            dimension_semantics=("parallel", "arbitrary"),
            vmem_limit_bytes=VMEM_LIMIT_BYTES),
        name="att_prep",
    )(x, w_att.astype(bf), g_cq.reshape(1, D_CQ), w_uq.T.astype(bf), w_iq.T.astype(bf),
      pad_vec(g_ik), pad_vec(b_ik), cos_a, sin_a, cos_i, sin_i, kc, ksa, ksb, ic, isa, isb)


def _attention(qt, qit, wit, k, vt, ki, topk):
    b, nq, dh, nht = qt.shape
    di = qit.shape[2]
    s = k.shape[1]
    t = s // nq
    nh = nht // t
    dhe = vt.shape[2]
    kern = functools.partial(_attn_kernel, t=t, nh=nh, topk=topk)
    return pl.pallas_call(
        kern,
        grid=(b, nq),
        in_specs=[
            pl.BlockSpec((1, 1, dh, nh * t), lambda bi, qb: (bi, qb, 0, 0)),
            pl.BlockSpec((1, 1, di, nh * t), lambda bi, qb: (bi, qb, 0, 0)),
            pl.BlockSpec((1, 1, 1, nh * t), lambda bi, qb: (bi, qb, 0, 0)),
            pl.BlockSpec((1, s, dh), lambda bi, qb: (bi, 0, 0)),
            pl.BlockSpec((1, nq, dhe, t), lambda bi, qb: (bi, 0, 0, 0)),
            pl.BlockSpec((1, s, di), lambda bi, qb: (bi, 0, 0)),
        ],
        out_specs=pl.BlockSpec((1, t, nh * dh), lambda bi, qb: (bi, qb, 0)),
        out_shape=jax.ShapeDtypeStruct((b, s, nh * dh), jnp.bfloat16),
        scratch_shapes=[
            pltpu.VMEM((nq, t, t), jnp.int32),
            pltpu.VMEM((1, t), jnp.int32),
            pltpu.VMEM((nh, t), jnp.float32),
            pltpu.VMEM((nh, dhe, t), jnp.float32),
        ],
        compiler_params=pltpu.CompilerParams(
            dimension_semantics=("parallel", "arbitrary"),
            vmem_limit_bytes=VMEM_LIMIT_BYTES),
        name="dsa_attention",
    )(qt, qit, wit, k, vt, ki)


def _bdot(a, b, dims=(((1,), (0,)), ((), ()))):
    return lax.dot_general(a.astype(jnp.bfloat16), b.astype(jnp.bfloat16), dims,
                           preferred_element_type=jnp.float32)


_NT = (((1,), (1,)), ((), ()))
_TN = (((0,), (0,)), ((), ()))


def _scan_kernel(r_ref, lw_ref, k_ref, v_ref, kk_ref, b_ref, y_ref, s_ref, *, chunk):
    @pl.when(pl.program_id(1) == 0)
    def _():
        s_ref[...] = jnp.zeros(s_ref.shape, jnp.float32)

    n_tiles = s_ref.shape[0]
    hl = LANES // 2
    gt = SCAN_GROUP_TILES
    rows = 2 * gt * chunk
    row = lax.broadcasted_iota(jnp.int32, (rows, rows), 0)
    col = lax.broadcasted_iota(jnp.int32, (rows, rows), 1)
    same = (row // chunk) == (col // chunk)
    incl = jnp.logical_and(same, row >= col)
    strict = jnp.logical_and(same, row > col)
    r1 = lax.broadcasted_iota(jnp.int32, (chunk, chunk), 0)
    c1 = lax.broadcasted_iota(jnp.int32, (chunk, chunk), 1)
    tril = jnp.where(r1 >= c1, 1.0, 0.0).astype(jnp.float32)
    lane = lax.broadcasted_iota(jnp.int32, (1, LANES), 1)
    even = lane < hl
    pr_ = lax.broadcasted_iota(jnp.int32, (LANES, LANES), 0)
    pc_ = lax.broadcasted_iota(jnp.int32, (LANES, LANES), 1)
    pair_bd = (pr_ // hl) == (pc_ // hl)

    lw = lw_ref[0]
    c = jnp.dot(tril, lw, precision=lax.Precision.HIGHEST,
                preferred_element_type=jnp.float32)
    e_neg = jnp.exp(-c)
    at = -kk_ref[0] * jnp.exp(c - lw)
    rt = r_ref[0] * jnp.exp(c)
    bt = b_ref[0] * e_neg
    kt = k_ref[0] * e_neg
    v = v_ref[0]
    p_last = jnp.exp(c[chunk - 1:chunk, :])
    bp = bt * p_last
    kp = kt * p_last

    def tile(x, t):
        return x[:, t * LANES:(t + 1) * LANES]

    def stack_masked(x, t0):
        parts = []
        for t in range(t0, t0 + gt):
            xt = tile(x, t)
            parts += [jnp.where(even, xt, 0.0), jnp.where(even, 0.0, xt)]
        return jnp.concatenate(parts, axis=0)

    def stack_plain(x, t0):
        return jnp.concatenate([tile(x, t) for t in range(t0, t0 + gt) for _ in range(2)], axis=0)

    steps = max(1, int(np.ceil(np.log2(chunk))))
    groups = list(range(0, n_tiles, gt))
    at_s = [stack_masked(at, t0) for t0 in groups]
    v_s = [stack_masked(v, t0) for t0 in groups]
    a_all = [_bdot(jnp.concatenate([at_s[g], stack_masked(rt, t0)], axis=0),
                   jnp.concatenate([stack_plain(bt, t0), stack_plain(kt, t0)], axis=0), _NT)
             for g, t0 in enumerate(groups)]
    mpow = [jnp.where(strict, a[:rows, :rows], 0.0) for a in a_all]
    a_ak = [jnp.where(strict, a[:rows, rows:], 0.0) for a in a_all]
    a_rb = [jnp.where(incl, a[rows:, :rows], 0.0) for a in a_all]
    a_rk = [jnp.where(incl, a[rows:, rows:], 0.0) for a in a_all]

    xs = [jnp.concatenate([at_s[g], _bdot(a_ak[g], v_s[g])], axis=1) for g in range(len(groups))]
    for i in range(steps):
        xs = [x + _bdot(m_, x) for x, m_ in zip(xs, mpow)]
        if i + 1 < steps:
            mpow = [_bdot(m_, m_) for m_ in mpow]
    qys = [_bdot(a_rb[g], xs[g]) for g in range(len(groups))]
    y0s = [qys[g][:, LANES:] + _bdot(a_rk[g], v_s[g]) for g in range(len(groups))]

    for g, t0 in enumerate(groups):
        x = xs[g]
        y0_s = y0s[g]
        qh_s = qys[g][:, :LANES]
        for j in range(gt):
            t = t0 + j
            lo, mid, hi = 2 * j * chunk, (2 * j + 1) * chunk, (2 * j + 2) * chunk
            unstack = lambda z: z[lo:mid] + z[mid:hi]
            x_t = unstack(x)
            qh_t = tile(rt, t) + unstack(qh_s)
            y0_t = unstack(y0_s)
            s0 = s_ref[t]
            gw = _bdot(x_t, tile(bp, t), _TN)
            g_t = jnp.where(pair_bd, gw[:LANES], 0.0)
            h_t = jnp.where(pair_bd, gw[LANES:] + _bdot(tile(v, t), tile(kp, t), _TN), 0.0)
            y_ref[0, :, t * LANES:(t + 1) * LANES] = y0_t + _bdot(qh_t, s0, _NT)
            s_ref[t] = s0 * tile(p_last, t) + _bdot(s0, g_t) + h_t


def _rwkv_scan(r, lw, k, v, kk, b, r_col=0, v_col=0):
    bsz, s, _ = lw.shape
    d = D_R
    chunk = min(SCAN_CHUNK, s)
    spec = pl.BlockSpec((1, chunk, d), lambda bi, ci: (bi, ci, 0))
    colspec = lambda j: pl.BlockSpec((1, chunk, d), lambda bi, ci, j=j: (bi, ci, j))
    kern = functools.partial(_scan_kernel, chunk=chunk)
    return pl.pallas_call(
        kern,
        grid=(bsz, s // chunk),
        in_specs=[colspec(r_col), spec, spec, colspec(v_col), spec, spec],
        out_specs=spec,
        out_shape=jax.ShapeDtypeStruct((bsz, s, d), jnp.float32),
        scratch_shapes=[pltpu.VMEM((d // LANES, LANES, LANES), jnp.float32)],
        compiler_params=pltpu.CompilerParams(
            dimension_semantics=("parallel", "arbitrary"),
            vmem_limit_bytes=VMEM_LIMIT_BYTES),
        name="rwkv7_scan",
    )(r, lw, k, v, kk, b)


RW_COL_LORA = 3 * D_R // LANES
RW_COL_GATE = RW_COL_LORA + 1
RW_COL_VRES = RW_COL_LORA + 2


def _rwprep_kernel(*refs, has_vres):
    if has_vres:
        (pk_ref, pv_ref, pwa_ref, pg_ref, pvr_ref, vf_ref, w0_ref, a0_ref, v0_ref, kk_ref, ka_ref,
         w2_ref, a2_ref, g2_ref, v2_ref, seg_ref, lw_ref, k_out, kkn_ref, b_ref, g_out, v_out) = refs
    else:
        (pk_ref, pwa_ref, pg_ref, w0_ref, a0_ref, kk_ref, ka_ref,
         w2_ref, a2_ref, g2_ref, seg_ref, lw_ref, k_out, kkn_ref, b_ref, g_out) = refs
    bf = jnp.bfloat16
    dotf = lambda a, w: jnp.dot(a.astype(bf), w, preferred_element_type=jnp.float32)
    pwa = pwa_ref[...]
    u = w0_ref[...] + dotf(jnp.tanh(pwa), w2_ref[...])
    neg = -u
    softplus = jnp.maximum(neg, 0.0) + jnp.log(1.0 + jnp.exp(-jnp.abs(neg)))
    lw_ref[...] = -jnp.exp(-softplus - 0.5)
    a = jax.nn.sigmoid(a0_ref[...] + dotf(pwa, a2_ref[...]))
    g_out[...] = dotf(jax.nn.sigmoid(pg_ref[...]), g2_ref[...])
    pk = pk_ref[...]
    kk = pk * kk_ref[...]
    norm = jnp.sqrt(_head_sum(kk * kk, seg_ref[...]))
    kk = kk / jnp.maximum(norm, 1e-12)
    kkn_ref[...] = kk
    b_ref[...] = kk * a
    k_out[...] = pk * (1.0 + (a - 1.0) * ka_ref[...])
    if has_vres:
        pv = pv_ref[...]
        mix = jax.nn.sigmoid(v0_ref[...] + dotf(pvr_ref[...], v2_ref[...]))
        v_out[...] = pv + (vf_ref[...] - pv) * mix


def _rw_prep(p_rw, v_first, w0, a0, v0, k_k, k_a, w2, a2, g2, v2):
    m = p_rw.shape[0]
    d = D_R
    tm = min(256, m)
    has_vres = v_first is not None
    bf = jnp.bfloat16
    hl = LANES // 2
    seg = (jnp.arange(LANES)[:, None] // hl == jnp.arange(LANES)[None, :] // hl).astype(bf)
    pad_rows = lambda w, top: jnp.concatenate(
        [jnp.zeros((top, d), w.dtype), w, jnp.zeros((LANES - top - w.shape[0], d), w.dtype)], axis=0).astype(bf)
    col = lambda j, width: pl.BlockSpec((tm, width), lambda i, j=j: (i, j))
    vec = pl.BlockSpec((1, d), lambda i: (0, 0))
    wspec = pl.BlockSpec((LANES, d), lambda i: (0, 0))
    segspec = pl.BlockSpec((LANES, LANES), lambda i: (0, 0))
    row_out = pl.BlockSpec((tm, d), lambda i: (i, 0))
    v1 = lambda t_: t_.reshape(1, d)
    if has_vres:
        args = [p_rw, p_rw, p_rw, p_rw, p_rw, v_first, v1(w0), v1(a0), v1(v0), v1(k_k), v1(k_a),
                pad_rows(w2, 0), pad_rows(a2, LORA_W), g2.astype(bf), pad_rows(v2, 0), seg]
        in_specs = [col(1, d), col(2, d), col(RW_COL_LORA, LANES), col(RW_COL_GATE, LANES),
                    col(RW_COL_VRES, LANES), col(2, d), vec, vec, vec, vec, vec,
                    wspec, wspec, wspec, wspec, segspec]
        n_out = 6
    else:
        args = [p_rw, p_rw, p_rw, v1(w0), v1(a0), v1(k_k), v1(k_a),
                pad_rows(w2, 0), pad_rows(a2, LORA_W), g2.astype(bf), seg]
        in_specs = [col(1, d), col(RW_COL_LORA, LANES), col(RW_COL_GATE, LANES), vec, vec, vec, vec,
                    wspec, wspec, wspec, segspec]
        n_out = 5
    return pl.pallas_call(
        functools.partial(_rwprep_kernel, has_vres=has_vres),
        grid=(m // tm,),
        in_specs=in_specs,
        out_specs=[row_out] * n_out,
        out_shape=[jax.ShapeDtypeStruct((m, d), jnp.float32)] * n_out,
        compiler_params=pltpu.CompilerParams(
            dimension_semantics=("parallel",),
            vmem_limit_bytes=VMEM_LIMIT_BYTES),
        name="rw_prep",
    )(*args)


def _head_sum(z, seg):
    hi = z.astype(jnp.bfloat16)
    lo = (z - hi.astype(jnp.float32)).astype(jnp.bfloat16)
    outs = []
    for t in range(z.shape[1] // LANES):
        sl = slice(t * LANES, (t + 1) * LANES)
        outs.append(jnp.dot(hi[:, sl], seg, preferred_element_type=jnp.float32)
                    + jnp.dot(lo[:, sl], seg, preferred_element_type=jnp.float32))
    return jnp.concatenate(outs, axis=1)


def _post_kernel(y_ref, r_ref, k_ref, v_ref, g_ref, o_ref, gate_ref, x_ref,
                 glnx_ref, blnx_ref, rk_ref, lng_ref, lnb_ref, seg_ref, wob_ref, woa_ref, wout_ref,
                 out_ref, *, alpha):
    seg = seg_ref[...]
    d_model = x_ref.shape[1]
    inv_n = 1.0 / N_R
    y = y_ref[...]
    dy = y - _head_sum(y, seg) * inv_n
    var = _head_sum(dy * dy, seg) * inv_n
    yn = dy * lax.rsqrt(var + GN_EPS) * glnx_ref[...] + blnx_ref[...]
    bonus = _head_sum(r_ref[...] * k_ref[...] * rk_ref[...], seg) * v_ref[...]
    yb_in = ((yn + bonus) * g_ref[...]).astype(jnp.bfloat16)
    y_b = jnp.dot(yb_in, wob_ref[...], preferred_element_type=jnp.float32)
    y_a = jnp.dot(o_ref[...], woa_ref[...], preferred_element_type=jnp.float32)
    gates = gate_ref[...]
    z = gates[:, :d_model] * y_a + gates[:, d_model:] * y_b
    t = alpha * x_ref[...] + jnp.dot(z.astype(jnp.bfloat16), wout_ref[...],
                                     preferred_element_type=jnp.float32)
    mu = jnp.mean(t, axis=-1, keepdims=True)
    dt = t - mu
    vt = jnp.mean(dt * dt, axis=-1, keepdims=True)
    out_ref[...] = dt * lax.rsqrt(vt + LN_EPS) * lng_ref[...] + lnb_ref[...]


def _mixer_post(y, r, k, v, g, o, gates, x, g_lnx, b_lnx, r_k, ln_g, ln_b, w_ob, w_oa, w_out, alpha, v_col=0):
    m, d = x.shape
    tm = min(256, m)
    hl = LANES // 2
    seg = (jnp.arange(LANES)[:, None] // hl == jnp.arange(LANES)[None, :] // hl).astype(jnp.bfloat16)
    row = lambda n, j=0: pl.BlockSpec((tm, n), lambda i: (i, j))
    vec = pl.BlockSpec((1, d), lambda i: (0, 0))
    full = lambda a, b_: pl.BlockSpec((a, b_), lambda i: (0, 0))
    bf = jnp.bfloat16
    return pl.pallas_call(
        functools.partial(_post_kernel, alpha=alpha),
        grid=(m // tm,),
        in_specs=[row(d), row(d), row(d), row(d, v_col), row(d), row(d), row(2 * d), row(d),
                  vec, vec, vec, vec, vec, full(LANES, LANES), full(d, d), full(d, d), full(d, d)],
        out_specs=row(d),
        out_shape=jax.ShapeDtypeStruct((m, d), jnp.float32),
        compiler_params=pltpu.CompilerParams(
            dimension_semantics=("parallel",),
            vmem_limit_bytes=VMEM_LIMIT_BYTES),
        name="mixer_post",
    )(y, r, k, v, g, o, gates, x,
      g_lnx.reshape(1, d), b_lnx.reshape(1, d), r_k.reshape(1, d), ln_g.reshape(1, d), ln_b.reshape(1, d),
      seg, w_ob.astype(bf), w_oa.astype(bf), w_out.astype(bf))


def kernel(x, positions, w_in0, w_in_rest, b_gate, g_cq, w_uq, w_iq, g_ik, b_ik, w_oa, mu_rwkv, mu_vres, w0, w2, a0, a2, v0, v2, g2, k_k, k_a, r_k, g_lnx, b_lnx, w_ob, w_out, ln1_g, ln1_b, w_up, conv_w, conv_b, w_down, ln2_g, ln2_b):
    bsz, seq, d = x.shape
    m = bsz * seq
    alpha = (2 * DEPTH) ** 0.25
    topk = min(TOPK_MAX, seq // 4)
    tabs_a, tabs_i = _rope_tables_all(positions)
    v_first = None

    for i in range(DEPTH):
        w_in = w_in0 if i == 0 else w_in_rest[i - 1]
        zcols = lambda n: jnp.zeros((d, n), w_in.dtype)
        w_att = jnp.concatenate([w_in[:, :N_ATT], zcols(N_ATT_PAD - N_ATT)], axis=1)
        n_rw = w_in.shape[1] - (N_ATT + N_GATE)
        w_rw = jnp.concatenate([w_in[:, N_ATT + N_GATE:], zcols(N_RW_PAD - n_rw)], axis=1)
        mu_parts = [mu_rwkv[i]] + ([mu_vres[i - 1]] if i > 0 else [])
        mu_rw = jnp.concatenate(mu_parts + [jnp.zeros((N_RW_PAD - n_rw,), jnp.float32)])
        gates = _mm_gate(x.reshape(m, d), w_in[:, N_ATT:N_ATT + N_GATE], b_gate[i]).reshape(bsz, seq, N_GATE)
        p_rw = _mm_shift(x, w_rw, mu_rw)

        o = _attention(*_att_prep(x, w_att, g_cq[i], w_uq[i], w_iq[i], g_ik[i], b_ik[i], tabs_a, tabs_i), topk)

        f2 = lambda t_: t_.reshape(m, t_.shape[-1])
        f3 = lambda t_: t_.reshape(bsz, seq, t_.shape[-1])
        v_col = 2
        if i == 0:
            v_first = p_rw
            lw, k_r, kk_n, b_r, g_r = _rw_prep(f2(p_rw), None, w0[i], a0[i], None, k_k[i], k_a[i],
                                               w2[i], a2[i], g2[i], None)
            v_r = p_rw
        else:
            lw, k_r, kk_n, b_r, g_r, v_r = _rw_prep(f2(p_rw), f2(v_first), w0[i], a0[i], v0[i - 1],
                                                    k_k[i], k_a[i], w2[i], a2[i], g2[i], v2[i - 1])
            v_r = f3(v_r)
            v_col = 0
        y = _rwkv_scan(p_rw, f3(lw), f3(k_r), v_r, f3(kk_n), f3(b_r), r_col=0, v_col=v_col)

        x = _mixer_post(f2(y), f2(p_rw), k_r, f2(v_r), g_r, f2(o), f2(gates), f2(x),
                        g_lnx[i], b_lnx[i], r_k[i], ln1_g[i], ln1_b[i],
                        w_ob[i], w_oa[i], w_out[i], alpha, v_col=v_col).reshape(bsz, seq, d)

        x = _ffn(x, w_up[i], conv_w[i], conv_b[i], w_down[i], ln2_g[i], ln2_b[i], alpha)
    return x
```

```python
import functools

import jax
import jax.numpy as jnp
import numpy as np
from jax import lax
from jax.experimental import pallas as pl
from jax.experimental.pallas import tpu as pltpu

D_MODEL = 1024
DEPTH = 4
H_A = 8
DH_A = 128
D_CQ = 256
ROT_A = DH_A // 4
H_IDX = 8
D_IDX = 64
ROT_IDX = D_IDX // 4
TOPK_MAX = 256
ROPE_THETA = 500000.0
N_R = 64
H_R = D_MODEL // N_R
D_R = H_R * N_R
LORA_W = 64
LORA_A = 64
LORA_V = 32
LORA_G = 128
GN_EPS = 64e-5
D_FF = ((8 * D_MODEL // 3 + 127) // 128) * 128
LN_EPS = 1e-5

N_ATT = D_CQ + 2 * DH_A + D_IDX + H_IDX
N_GATE = 2 * D_MODEL
N_RWKV = 3 * D_R + LORA_W + LORA_A + LORA_G
N_IN0 = N_ATT + N_GATE + N_RWKV

LANES = 128
SUBLANES = 8
LOG2E = 1.4426950408889634
VMEM_LIMIT_BYTES = 48 * 1024 * 1024

N_ATT_PAD = 640
N_RW_PAD = 3584

ATTN_TILE = 256
SCAN_CHUNK = 64
SCAN_GROUP_TILES = 1
NEG_BIG = -1e30
INT_MIN = -(2 ** 31)


def _pick_tile(n, cap):
    best = None
    for t in range(LANES, min(n, cap) + 1, LANES):
        if n % t == 0:
            best = t
    return best if best is not None else n


def _gate_kernel(x_ref, w_ref, b_ref, o_ref):
    p = jnp.dot(x_ref[...].astype(jnp.bfloat16), w_ref[...], preferred_element_type=jnp.float32)
    o_ref[...] = jax.nn.sigmoid(p + b_ref[...])


def _mm_gate(x, w, bias):
    m, k = x.shape
    n = w.shape[1]
    tm = min(1024, m)
    tn = _pick_tile(n, 1024)
    return pl.pallas_call(
        _gate_kernel,
        grid=(m // tm, n // tn),
        in_specs=[pl.BlockSpec((tm, k), lambda i, j: (i, 0)),
                  pl.BlockSpec((k, tn), lambda i, j: (0, j)),
                  pl.BlockSpec((1, tn), lambda i, j: (0, j))],
        out_specs=pl.BlockSpec((tm, tn), lambda i, j: (i, j)),
        out_shape=jax.ShapeDtypeStruct((m, n), jnp.float32),
        compiler_params=pltpu.CompilerParams(
            dimension_semantics=("parallel", "arbitrary"),
            vmem_limit_bytes=VMEM_LIMIT_BYTES),
        name="mm_gate",
    )(x, w.astype(jnp.bfloat16), bias.reshape(1, n))


def _shift_rows(cur, prev_rows, k):
    rolled = pltpu.roll(cur, k, 0)
    row = lax.broadcasted_iota(jnp.int32, cur.shape, 0)
    for i in range(k):
        rolled = jnp.where(row == i, prev_rows[i:i + 1, :], rolled)
    return rolled


def _shift_mm_kernel(x_ref, w_ref, mu_ref, o_ref, carry_ref):
    si = pl.program_id(1)
    j = pl.program_id(2)
    tm = o_ref.shape[1]
    p = jnp.dot(x_ref[0].astype(jnp.bfloat16), w_ref[...], preferred_element_type=jnp.float32)
    @pl.when(si == 0)
    def _():
        carry_ref[j] = jnp.zeros(carry_ref.shape[1:], jnp.float32)

    prev = _shift_rows(p, carry_ref[j, 0:1, :], 1)
    carry_ref[j, 0:1, :] = p[tm - 1:tm, :]
    o_ref[0] = p + (prev - p) * mu_ref[...]


def _mm_shift(x, w, mu):
    b, s, k = x.shape
    n = w.shape[1]
    tm = min(1024, s)
    tn = 512 if n % 512 == 0 else _pick_tile(n, 1024)
    return pl.pallas_call(
        _shift_mm_kernel,
        grid=(b, s // tm, n // tn),
        in_specs=[pl.BlockSpec((1, tm, k), lambda bi, si, j: (bi, si, 0)),
                  pl.BlockSpec((k, tn), lambda bi, si, j: (0, j)),
                  pl.BlockSpec((1, tn), lambda bi, si, j: (0, j))],
        out_specs=pl.BlockSpec((1, tm, tn), lambda bi, si, j: (bi, si, j)),
        out_shape=jax.ShapeDtypeStruct((b, s, n), jnp.float32),
        scratch_shapes=[pltpu.VMEM((n // tn, 8, tn), jnp.float32)],
        compiler_params=pltpu.CompilerParams(
            dimension_semantics=("parallel", "arbitrary", "arbitrary"),
            vmem_limit_bytes=VMEM_LIMIT_BYTES),
        name="mm_shift",
    )(x, w.astype(jnp.bfloat16), mu.reshape(1, n))


def _ffn_kernel(x_ref, wg_ref, wv_ref, cw_ref, cb_ref, wd_ref, g_ref, b_ref, o_ref,
                xb_ref, acc_ref, carry_ref, *, alpha, eps):
    si = pl.program_id(1)
    c = pl.program_id(2)
    tm = o_ref.shape[1]

    @pl.when(c == 0)
    def _():
        xb_ref[...] = x_ref[0].astype(jnp.bfloat16)
        acc_ref[...] = jnp.zeros(acc_ref.shape, jnp.float32)

    xb = xb_ref[...]
    hg = jnp.dot(xb, wg_ref[...], preferred_element_type=jnp.float32)
    hv = jnp.dot(xb, wv_ref[...], preferred_element_type=jnp.float32)
    @pl.when(si == 0)
    def _():
        carry_ref[c] = jnp.zeros(carry_ref.shape[1:], jnp.float32)

    tail = carry_ref[c, 0:2, :]
    g1 = _shift_rows(hg, tail[1:2, :], 1)
    g2 = _shift_rows(hg, tail, 2)
    carry_ref[c, 0:2, :] = hg[tm - 2:tm, :]
    cw = cw_ref[...]
    conv = g2 * cw[0:1, :] + g1 * cw[1:2, :] + hg * cw[2:3, :] + cb_ref[...]
    act = jax.nn.silu(conv) * hv
    acc_ref[...] += jnp.dot(act.astype(jnp.bfloat16), wd_ref[...], preferred_element_type=jnp.float32)

    @pl.when(c == pl.num_programs(2) - 1)
    def _():
        y = alpha * x_ref[0] + acc_ref[...]
        mu = jnp.mean(y, axis=-1, keepdims=True)
        d = y - mu
        var = jnp.mean(d * d, axis=-1, keepdims=True)
        o_ref[0] = d * lax.rsqrt(var + eps) * g_ref[...] + b_ref[...]


def _ffn(x, w_up, conv_w, conv_b, w_down, ln_g, ln_b, alpha):
    b, s, d = x.shape
    dff = w_down.shape[0]
    tm = min(1024, s)
    tf = 256 if dff % 256 == 0 else LANES
    nc = dff // tf
    kern = functools.partial(_ffn_kernel, alpha=alpha, eps=LN_EPS)
    w_up = w_up.astype(jnp.bfloat16)
    return pl.pallas_call(
        kern,
        grid=(b, s // tm, nc),
        in_specs=[
            pl.BlockSpec((1, tm, d), lambda bi, si, c: (bi, si, 0)),
            pl.BlockSpec((d, tf), lambda bi, si, c: (0, c)),
            pl.BlockSpec((d, tf), lambda bi, si, c: (0, c + nc)),
            pl.BlockSpec((3, tf), lambda bi, si, c: (0, c)),
            pl.BlockSpec((1, tf), lambda bi, si, c: (0, c)),
            pl.BlockSpec((tf, d), lambda bi, si, c: (c, 0)),
            pl.BlockSpec((1, d), lambda bi, si, c: (0, 0)),
            pl.BlockSpec((1, d), lambda bi, si, c: (0, 0)),
        ],
        out_specs=pl.BlockSpec((1, tm, d), lambda bi, si, c: (bi, si, 0)),
        out_shape=jax.ShapeDtypeStruct((b, s, d), jnp.float32),
        scratch_shapes=[
            pltpu.VMEM((tm, d), jnp.bfloat16),
            pltpu.VMEM((tm, d), jnp.float32),
            pltpu.VMEM((nc, 8, tf), jnp.float32),
        ],
        compiler_params=pltpu.CompilerParams(
            dimension_semantics=("parallel", "arbitrary", "arbitrary"),
            vmem_limit_bytes=VMEM_LIMIT_BYTES),
        name="ffn",
    )(x, w_up, w_up, conv_w, conv_b.reshape(1, dff), w_down.astype(jnp.bfloat16),
      ln_g.reshape(1, d), ln_b.reshape(1, d))


def _sortable_key(score):
    score = jnp.where(score == 0.0, 0.0, score)
    bits = pltpu.bitcast(score, jnp.int32)
    return jnp.where(bits < 0, bits ^ jnp.int32(0x7FFFFFFF), bits)


def _attn_kernel(qt_ref, qit_ref, wi_ref, k_ref, vt_ref, ki_ref, o_ref,
                 key_ref, jthr_ref, m_ref, acc_ref, *, t, nh, topk):
    qb = pl.program_id(1)
    n_kt = qb + 1
    krow = lax.broadcasted_iota(jnp.int32, (t, t), 0)
    qcol = lax.broadcasted_iota(jnp.int32, (t, t), 1)

    qit = qit_ref[0, 0]
    wi = wi_ref[0, 0]

    def score_tile(kt, carry):
        ki_t = ki_ref[0, pl.ds(kt * t, t), :]
        rel = jnp.dot(ki_t, qit, preferred_element_type=jnp.float32)
        rel = jnp.maximum(rel, 0.0) * wi
        score = rel[:, 0:t]
        for h in range(1, nh):
            score = score + rel[:, h * t:(h + 1) * t]
        score = jnp.where(jnp.logical_and(kt == qb, krow > qcol), -jnp.inf, score)
        key_ref[kt] = _sortable_key(score)
        return carry

    lax.fori_loop(0, n_kt, score_tile, 0)

    def count(pred_fn):
        def body(kt, accs):
            hit = pred_fn(key_ref[kt], kt)
            accs = list(accs)
            for j in range(t // 8):
                a = accs[j % len(accs)]
                accs[j % len(accs)] = jnp.where(hit[j * 8:(j + 1) * 8], a + 1, a)
            return tuple(accs)
        accs = lax.fori_loop(0, n_kt, body, tuple(jnp.zeros((8, t), jnp.int32) for _ in range(4)))
        return jnp.sum(accs[0] + accs[1] + accs[2] + accs[3], axis=0, keepdims=True)

    def count_ge(cand):
        return count(lambda kk, kt: kk >= cand)

    cnt0 = count_ge(jnp.zeros((1, t), jnp.int32))
    nonneg = cnt0 >= topk
    thr0 = jnp.where(nonneg, jnp.int32(0), jnp.int32(INT_MIN))
    n_ge0 = jnp.where(nonneg, cnt0, n_kt * t)

    def bit_step(i, state):
        thr, n_ge = state
        cand = thr | jnp.left_shift(jnp.int32(1), 30 - i)
        cnt = count_ge(cand)
        take = cnt >= topk
        return jnp.where(take, cand, thr), jnp.where(take, cnt, n_ge)

    thr, n_ge = lax.fori_loop(0, 31, bit_step, (thr0, n_ge0))
    jthr_ref[...] = jnp.full(jthr_ref.shape, 2 ** 30, jnp.int32)

    @pl.when(jnp.max(n_ge) > topk)
    def _():
        need = topk - count(lambda kk, kt: kk > thr)
        idx_bits = int(np.ceil(np.log2(key_ref.shape[0] * t)))

        def idx_step(i, j):
            cand = j | jnp.left_shift(jnp.int32(1), idx_bits - 1 - i)
            below = count(lambda kk, kt: jnp.logical_and(kk == thr, krow + kt * t < cand))
            return jnp.where(below < need, cand, j)

        jthr_ref[...] = lax.fori_loop(0, idx_bits, idx_step, jnp.zeros((1, t), jnp.int32))

    qt = qt_ref[0, 0]
    m_ref[...] = jnp.full(m_ref.shape, NEG_BIG, jnp.float32)
    acc_ref[...] = jnp.zeros(acc_ref.shape, jnp.float32)
    jthr = jthr_ref[...]

    def attn_tile(kt, carry):
        kk = key_ref[kt]
        gk = krow + kt * t
        sel = jnp.logical_or(kk > thr, jnp.logical_and(kk == thr, gk <= jthr))
        sel = jnp.logical_and(sel, gk <= qcol + qb * t)
        bias = jnp.where(sel, 0.0, NEG_BIG)
        k_t = k_ref[0, pl.ds(kt * t, t), :]
        vt_t = vt_ref[0, kt]
        s_all = jnp.dot(k_t, qt, preferred_element_type=jnp.float32)
        for h in range(nh):
            s = s_all[:, h * t:(h + 1) * t] + bias
            m_old = m_ref[h:h + 1, :]
            m_new = jnp.maximum(m_old, jnp.max(s, axis=0, keepdims=True))
            alpha = jnp.exp2(m_old - m_new)
            p = jnp.exp2(s - m_new)
            acc_ref[h] = acc_ref[h] * alpha + jnp.dot(
                vt_t, p.astype(jnp.bfloat16), preferred_element_type=jnp.float32)
            m_ref[h:h + 1, :] = m_new
        return carry

    lax.fori_loop(0, n_kt, attn_tile, 0)
    dh = o_ref.shape[2] // nh
    for h in range(nh):
        acc = acc_ref[h]
        out = acc[:dh] / acc[dh:dh + 1]
        o_ref[0, :, h * dh:(h + 1) * dh] = out.T.astype(o_ref.dtype)


def _rope_rows(blk, cos, sin, half):
    x1, x2 = blk[:half], blk[half:2 * half]
    return jnp.concatenate([x1 * cos - x2 * sin, x2 * cos + x1 * sin, blk[2 * half:]], axis=0)


def _rope_lanes(x, c, sa, sb, half):
    return x * c + pltpu.roll(x, LANES - half, 1) * sa + pltpu.roll(x, half, 1) * sb


def _attprep_kernel(x_ref, watt_ref, gcq_ref, wuqt_ref, wiqt_ref, gik_ref, bik_ref,
                    cosa_ref, sina_ref, cosi_ref, sini_ref,
                    kc_ref, ksa_ref, ksb_ref, ic_ref, isa_ref, isb_ref,
                    qt_ref, qit_ref, wit_ref, k_ref, vt_ref, ki_ref, *, t, nh, q_scale, w_scale):
    bf = jnp.bfloat16
    nt = (((1,), (1,)), ((), ()))
    p = jnp.dot(x_ref[0].astype(bf), watt_ref[...], preferred_element_type=jnp.float32)
    cq = p[:, :D_CQ]
    c_q = cq * lax.rsqrt(jnp.mean(cq * cq, axis=-1, keepdims=True) + 1e-6) * gcq_ref[...]
    cqb = c_q.astype(bf)

    q_t = lax.dot_general(wuqt_ref[...], cqb, nt, preferred_element_type=jnp.float32)
    cos, sin = cosa_ref[0], sina_ref[0]
    for h in range(nh):
        blk = _rope_rows(q_t[h * DH_A:(h + 1) * DH_A], cos, sin, ROT_A // 2) * q_scale
        qt_ref[0, 0, :, h * t:(h + 1) * t] = blk.astype(bf)
    qi_t = lax.dot_general(wiqt_ref[...], cqb, nt, preferred_element_type=jnp.float32)
    cos, sin = cosi_ref[0], sini_ref[0]
    for h in range(nh):
        blk = _rope_rows(qi_t[h * D_IDX:(h + 1) * D_IDX], cos, sin, ROT_IDX // 2)
        qit_ref[0, 0, :, h * t:(h + 1) * t] = blk.astype(bf)

    k_raw = p[:, D_CQ:D_CQ + DH_A]
    k_ref[0] = _rope_lanes(k_raw, kc_ref[0], ksa_ref[0], ksb_ref[0], ROT_A // 2).astype(bf)
    v_raw = p[:, D_CQ + DH_A:D_CQ + 2 * DH_A]
    vt_ref[0, 0, :DH_A, :] = v_raw.T.astype(bf)
    vt_ref[0, 0, DH_A:, :] = jnp.ones((vt_ref.shape[2] - DH_A, t), bf)

    tail = p[:, D_CQ + 2 * DH_A:]
    lane = lax.broadcasted_iota(jnp.int32, (1, LANES), 1)
    is_key = lane < D_IDX
    mu = jnp.sum(jnp.where(is_key, tail, 0.0), axis=-1, keepdims=True) * (1.0 / D_IDX)
    dk = jnp.where(is_key, tail - mu, 0.0)
    var = jnp.sum(dk * dk, axis=-1, keepdims=True) * (1.0 / D_IDX)
    ln = dk * lax.rsqrt(var + LN_EPS) * gik_ref[...] + bik_ref[...]
    ki = _rope_lanes(ln, ic_ref[0], isa_ref[0], isb_ref[0], ROT_IDX // 2)
    ki_ref[0] = ki[:, :D_IDX].astype(bf)
    w_t = tail.T[D_IDX:D_IDX + nh] * w_scale
    for h in range(nh):
        wit_ref[0, 0, :, h * t:(h + 1) * t] = w_t[h:h + 1]


def _rope_tables_all(positions):
    def tables(rot):
        half = rot // 2
        inv_freq = jnp.power(jnp.float32(ROPE_THETA), -jnp.arange(0, rot, 2, dtype=jnp.float32) / rot)
        ang = positions.astype(jnp.float32)[..., None] * inv_freq
        cos, sin = jnp.cos(ang), jnp.sin(ang)
        ones = jnp.ones(ang.shape[:2] + (LANES - 2 * half,), jnp.float32)
        zeros = lambda n: jnp.zeros(ang.shape[:2] + (n,), jnp.float32)
        c = jnp.concatenate([cos, cos, ones], axis=-1)
        sa = jnp.concatenate([-sin, zeros(LANES - half)], axis=-1)
        sb = jnp.concatenate([zeros(half), sin, zeros(LANES - 2 * half)], axis=-1)
        return cos.swapaxes(1, 2), sin.swapaxes(1, 2), c, sa, sb
    return tables(ROT_A), tables(ROT_IDX)


def _att_prep(x, w_att, g_cq, w_uq, w_iq, g_ik, b_ik, tabs_a, tabs_i):
    b, s, d = x.shape
    t = min(ATTN_TILE, s)
    nq = s // t
    nh = H_A
    bf = jnp.bfloat16
    dhe = DH_A + SUBLANES
    pad_vec = lambda vv: jnp.concatenate([vv, jnp.zeros((LANES - vv.shape[0],), vv.dtype)]).reshape(1, LANES)
    cos_a, sin_a, kc, ksa, ksb = tabs_a
    cos_i, sin_i, ic, isa, isb = tabs_i
    const = lambda a, b_: pl.BlockSpec((a, b_), lambda bi, qb: (0, 0))
    fm = lambda rows: pl.BlockSpec((1, rows, t), lambda bi, qb: (bi, 0, qb))
    tm_ = pl.BlockSpec((1, t, LANES), lambda bi, qb: (bi, qb, 0))
    kern = functools.partial(_attprep_kernel, t=t, nh=nh,
                             q_scale=float(DH_A) ** -0.5 * LOG2E, w_scale=H_IDX ** -0.5 * D_IDX ** -0.5)
    return pl.pallas_call(
        kern,
        grid=(b, nq),
        in_specs=[
            pl.BlockSpec((1, t, d), lambda bi, qb: (bi, qb, 0)),
            const(d, N_ATT_PAD), const(1, D_CQ), const(nh * DH_A, D_CQ), const(nh * D_IDX, D_CQ),
            const(1, LANES), const(1, LANES),
            fm(ROT_A // 2), fm(ROT_A // 2), fm(ROT_IDX // 2), fm(ROT_IDX // 2),
            tm_, tm_, tm_, tm_, tm_, tm_,
        ],
        out_specs=[
            pl.BlockSpec((1, 1, DH_A, nh * t), lambda bi, qb: (bi, qb, 0, 0)),
            pl.BlockSpec((1, 1, D_IDX, nh * t), lambda bi, qb: (bi, qb, 0, 0)),
            pl.BlockSpec((1, 1, 1, nh * t), lambda bi, qb: (bi, qb, 0, 0)),
            pl.BlockSpec((1, t, DH_A), lambda bi, qb: (bi, qb, 0)),
            pl.BlockSpec((1, 1, dhe, t), lambda bi, qb: (bi, qb, 0, 0)),
            pl.BlockSpec((1, t, D_IDX), lambda bi, qb: (bi, qb, 0)),
        ],
        out_shape=[
            jax.ShapeDtypeStruct((b, nq, DH_A, nh * t), bf),
            jax.ShapeDtypeStruct((b, nq, D_IDX, nh * t), bf),
            jax.ShapeDtypeStruct((b, nq, 1, nh * t), jnp.float32),
            jax.ShapeDtypeStruct((b, s, DH_A), bf),
            jax.ShapeDtypeStruct((b, nq, dhe, t), bf),
            jax.ShapeDtypeStruct((b, s, D_IDX), bf),
        ],
        compiler_params=pltpu.CompilerParams(
            dimension_semantics=("parallel", "arbitrary"),
            vmem_limit_bytes=VMEM_LIMIT_BYTES),
        name="att_prep",
    )(x, w_att.astype(bf), g_cq.reshape(1, D_CQ), w_uq.T.astype(bf), w_iq.T.astype(bf),
      pad_vec(g_ik), pad_vec(b_ik), cos_a, sin_a, cos_i, sin_i, kc, ksa, ksb, ic, isa, isb)


def _attention(qt, qit, wit, k, vt, ki, topk):
    b, nq, dh, nht = qt.shape
    di = qit.shape[2]
    s = k.shape[1]
    t = s // nq
    nh = nht // t
    dhe = vt.shape[2]
    kern = functools.partial(_attn_kernel, t=t, nh=nh, topk=topk)
    return pl.pallas_call(
        kern,
        grid=(b, nq),
        in_specs=[
            pl.BlockSpec((1, 1, dh, nh * t), lambda bi, qb: (bi, qb, 0, 0)),
            pl.BlockSpec((1, 1, di, nh * t), lambda bi, qb: (bi, qb, 0, 0)),
            pl.BlockSpec((1, 1, 1, nh * t), lambda bi, qb: (bi, qb, 0, 0)),
            pl.BlockSpec((1, s, dh), lambda bi, qb: (bi, 0, 0)),
            pl.BlockSpec((1, nq, dhe, t), lambda bi, qb: (bi, 0, 0, 0)),
            pl.BlockSpec((1, s, di), lambda bi, qb: (bi, 0, 0)),
        ],
        out_specs=pl.BlockSpec((1, t, nh * dh), lambda bi, qb: (bi, qb, 0)),
        out_shape=jax.ShapeDtypeStruct((b, s, nh * dh), jnp.bfloat16),
        scratch_shapes=[
            pltpu.VMEM((nq, t, t), jnp.int32),
            pltpu.VMEM((1, t), jnp.int32),
            pltpu.VMEM((nh, t), jnp.float32),
            pltpu.VMEM((nh, dhe, t), jnp.float32),
        ],
        compiler_params=pltpu.CompilerParams(
            dimension_semantics=("parallel", "arbitrary"),
            vmem_limit_bytes=VMEM_LIMIT_BYTES),
        name="dsa_attention",
    )(qt, qit, wit, k, vt, ki)


def _bdot(a, b, dims=(((1,), (0,)), ((), ()))):
    return lax.dot_general(a.astype(jnp.bfloat16), b.astype(jnp.bfloat16), dims,
                           preferred_element_type=jnp.float32)


_NT = (((1,), (1,)), ((), ()))
_TN = (((0,), (0,)), ((), ()))


def _scan_kernel(r_ref, lw_ref, k_ref, v_ref, kk_ref, b_ref, y_ref, s_ref, *, chunk):
    @pl.when(pl.program_id(1) == 0)
    def _():
        s_ref[...] = jnp.zeros(s_ref.shape, jnp.float32)

    n_tiles = s_ref.shape[0]
    hl = LANES // 2
    gt = SCAN_GROUP_TILES
    rows = 2 * gt * chunk
    row = lax.broadcasted_iota(jnp.int32, (rows, rows), 0)
    col = lax.broadcasted_iota(jnp.int32, (rows, rows), 1)
    same = (row // chunk) == (col // chunk)
    incl = jnp.logical_and(same, row >= col)
    strict = jnp.logical_and(same, row > col)
    r1 = lax.broadcasted_iota(jnp.int32, (chunk, chunk), 0)
    c1 = lax.broadcasted_iota(jnp.int32, (chunk, chunk), 1)
    tril = jnp.where(r1 >= c1, 1.0, 0.0).astype(jnp.float32)
    lane = lax.broadcasted_iota(jnp.int32, (1, LANES), 1)
    even = lane < hl
    pr_ = lax.broadcasted_iota(jnp.int32, (LANES, LANES), 0)
    pc_ = lax.broadcasted_iota(jnp.int32, (LANES, LANES), 1)
    pair_bd = (pr_ // hl) == (pc_ // hl)

    lw = lw_ref[0]
    c = jnp.dot(tril, lw, precision=lax.Precision.HIGHEST,
                preferred_element_type=jnp.float32)
    e_neg = jnp.exp(-c)
    at = -kk_ref[0] * jnp.exp(c - lw)
    rt = r_ref[0] * jnp.exp(c)
    bt = b_ref[0] * e_neg
    kt = k_ref[0] * e_neg
    v = v_ref[0]
    p_last = jnp.exp(c[chunk - 1:chunk, :])
    bp = bt * p_last
    kp = kt * p_last

    def tile(x, t):
        return x[:, t * LANES:(t + 1) * LANES]

    def stack_masked(x, t0):
        parts = []
        for t in range(t0, t0 + gt):
            xt = tile(x, t)
            parts += [jnp.where(even, xt, 0.0), jnp.where(even, 0.0, xt)]
        return jnp.concatenate(parts, axis=0)

    def stack_plain(x, t0):
        return jnp.concatenate([tile(x, t) for t in range(t0, t0 + gt) for _ in range(2)], axis=0)

    steps = max(1, int(np.ceil(np.log2(chunk))))
    groups = list(range(0, n_tiles, gt))
    at_s = [stack_masked(at, t0) for t0 in groups]
    v_s = [stack_masked(v, t0) for t0 in groups]
    a_all = [_bdot(jnp.concatenate([at_s[g], stack_masked(rt, t0)], axis=0),
                   jnp.concatenate([stack_plain(bt, t0), stack_plain(kt, t0)], axis=0), _NT)
             for g, t0 in enumerate(groups)]
    mpow = [jnp.where(strict, a[:rows, :rows], 0.0) for a in a_all]
    a_ak = [jnp.where(strict, a[:rows, rows:], 0.0) for a in a_all]
    a_rb = [jnp.where(incl, a[rows:, :rows], 0.0) for a in a_all]
    a_rk = [jnp.where(incl, a[rows:, rows:], 0.0) for a in a_all]

    xs = [jnp.concatenate([at_s[g], _bdot(a_ak[g], v_s[g])], axis=1) for g in range(len(groups))]
    for i in range(steps):
        xs = [x + _bdot(m_, x) for x, m_ in zip(xs, mpow)]
        if i + 1 < steps:
            mpow = [_bdot(m_, m_) for m_ in mpow]
    qys = [_bdot(a_rb[g], xs[g]) for g in range(len(groups))]
    y0s = [qys[g][:, LANES:] + _bdot(a_rk[g], v_s[g]) for g in range(len(groups))]

    for g, t0 in enumerate(groups):
        x = xs[g]
        y0_s = y0s[g]
        qh_s = qys[g][:, :LANES]
        for j in range(gt):
            t = t0 + j
            lo, mid, hi = 2 * j * chunk, (2 * j + 1) * chunk, (2 * j + 2) * chunk
            unstack = lambda z: z[lo:mid] + z[mid:hi]
            x_t = unstack(x)
            qh_t = tile(rt, t) + unstack(qh_s)
            y0_t = unstack(y0_s)
            s0 = s_ref[t]
            gw = _bdot(x_t, tile(bp, t), _TN)
            g_t = jnp.where(pair_bd, gw[:LANES], 0.0)
            h_t = jnp.where(pair_bd, gw[LANES:] + _bdot(tile(v, t), tile(kp, t), _TN), 0.0)
            y_ref[0, :, t * LANES:(t + 1) * LANES] = y0_t + _bdot(qh_t, s0, _NT)
            s_ref[t] = s0 * tile(p_last, t) + _bdot(s0, g_t) + h_t


def _rwkv_scan(r, lw, k, v, kk, b, r_col=0, v_col=0):
    bsz, s, _ = lw.shape
    d = D_R
    chunk = min(SCAN_CHUNK, s)
    spec = pl.BlockSpec((1, chunk, d), lambda bi, ci: (bi, ci, 0))
    colspec = lambda j: pl.BlockSpec((1, chunk, d), lambda bi, ci, j=j: (bi, ci, j))
    kern = functools.partial(_scan_kernel, chunk=chunk)
    return pl.pallas_call(
        kern,
        grid=(bsz, s // chunk),
        in_specs=[colspec(r_col), spec, spec, colspec(v_col), spec, spec],
        out_specs=spec,
        out_shape=jax.ShapeDtypeStruct((bsz, s, d), jnp.float32),
        scratch_shapes=[pltpu.VMEM((d // LANES, LANES, LANES), jnp.float32)],
        compiler_params=pltpu.CompilerParams(
            dimension_semantics=("parallel", "arbitrary"),
            vmem_limit_bytes=VMEM_LIMIT_BYTES),
        name="rwkv7_scan",
    )(r, lw, k, v, kk, b)


RW_COL_LORA = 3 * D_R // LANES
RW_COL_GATE = RW_COL_LORA + 1
RW_COL_VRES = RW_COL_LORA + 2


def _rwprep_kernel(*refs, has_vres):
    if has_vres:
        (pk_ref, pv_ref, pwa_ref, pg_ref, pvr_ref, vf_ref, w0_ref, a0_ref, v0_ref, kk_ref, ka_ref,
         w2_ref, a2_ref, g2_ref, v2_ref, seg_ref, lw_ref, k_out, kkn_ref, b_ref, g_out, v_out) = refs
    else:
        (pk_ref, pwa_ref, pg_ref, w0_ref, a0_ref, kk_ref, ka_ref,
         w2_ref, a2_ref, g2_ref, seg_ref, lw_ref, k_out, kkn_ref, b_ref, g_out) = refs
    bf = jnp.bfloat16
    dotf = lambda a, w: jnp.dot(a.astype(bf), w, preferred_element_type=jnp.float32)
    pwa = pwa_ref[...]
    u = w0_ref[...] + dotf(jnp.tanh(pwa), w2_ref[...])
    neg = -u
    softplus = jnp.maximum(neg, 0.0) + jnp.log(1.0 + jnp.exp(-jnp.abs(neg)))
    lw_ref[...] = -jnp.exp(-softplus - 0.5)
    a = jax.nn.sigmoid(a0_ref[...] + dotf(pwa, a2_ref[...]))
    g_out[...] = dotf(jax.nn.sigmoid(pg_ref[...]), g2_ref[...])
    pk = pk_ref[...]
    kk = pk * kk_ref[...]
    norm = jnp.sqrt(_head_sum(kk * kk, seg_ref[...]))
    kk = kk / jnp.maximum(norm, 1e-12)
    kkn_ref[...] = kk
    b_ref[...] = kk * a
    k_out[...] = pk * (1.0 + (a - 1.0) * ka_ref[...])
    if has_vres:
        pv = pv_ref[...]
        mix = jax.nn.sigmoid(v0_ref[...] + dotf(pvr_ref[...], v2_ref[...]))
        v_out[...] = pv + (vf_ref[...] - pv) * mix


def _rw_prep(p_rw, v_first, w0, a0, v0, k_k, k_a, w2, a2, g2, v2):
    m = p_rw.shape[0]
    d = D_R
    tm = min(256, m)
    has_vres = v_first is not None
    bf = jnp.bfloat16
    hl = LANES // 2
    seg = (jnp.arange(LANES)[:, None] // hl == jnp.arange(LANES)[None, :] // hl).astype(bf)
    pad_rows = lambda w, top: jnp.concatenate(
        [jnp.zeros((top, d), w.dtype), w, jnp.zeros((LANES - top - w.shape[0], d), w.dtype)], axis=0).astype(bf)
    col = lambda j, width: pl.BlockSpec((tm, width), lambda i, j=j: (i, j))
    vec = pl.BlockSpec((1, d), lambda i: (0, 0))
    wspec = pl.BlockSpec((LANES, d), lambda i: (0, 0))
    segspec = pl.BlockSpec((LANES, LANES), lambda i: (0, 0))
    row_out = pl.BlockSpec((tm, d), lambda i: (i, 0))
    v1 = lambda t_: t_.reshape(1, d)
    if has_vres:
        args = [p_rw, p_rw, p_rw, p_rw, p_rw, v_first, v1(w0), v1(a0), v1(v0), v1(k_k), v1(k_a),
                pad_rows(w2, 0), pad_rows(a2, LORA_W), g2.astype(bf), pad_rows(v2, 0), seg]
        in_specs = [col(1, d), col(2, d), col(RW_COL_LORA, LANES), col(RW_COL_GATE, LANES),
                    col(RW_COL_VRES, LANES), col(2, d), vec, vec, vec, vec, vec,
                    wspec, wspec, wspec, wspec, segspec]
        n_out = 6
    else:
        args = [p_rw, p_rw, p_rw, v1(w0), v1(a0), v1(k_k), v1(k_a),
                pad_rows(w2, 0), pad_rows(a2, LORA_W), g2.astype(bf), seg]
        in_specs = [col(1, d), col(RW_COL_LORA, LANES), col(RW_COL_GATE, LANES), vec, vec, vec, vec,
                    wspec, wspec, wspec, segspec]
        n_out = 5
    return pl.pallas_call(
        functools.partial(_rwprep_kernel, has_vres=has_vres),
        grid=(m // tm,),
        in_specs=in_specs,
        out_specs=[row_out] * n_out,
        out_shape=[jax.ShapeDtypeStruct((m, d), jnp.float32)] * n_out,
        compiler_params=pltpu.CompilerParams(
            dimension_semantics=("parallel",),
            vmem_limit_bytes=VMEM_LIMIT_BYTES),
        name="rw_prep",
    )(*args)


def _head_sum(z, seg):
    hi = z.astype(jnp.bfloat16)
    lo = (z - hi.astype(jnp.float32)).astype(jnp.bfloat16)
    outs = []
    for t in range(z.shape[1] // LANES):
        sl = slice(t * LANES, (t + 1) * LANES)
        outs.append(jnp.dot(hi[:, sl], seg, preferred_element_type=jnp.float32)
                    + jnp.dot(lo[:, sl], seg, preferred_element_type=jnp.float32))
    return jnp.concatenate(outs, axis=1)


def _post_kernel(y_ref, r_ref, k_ref, v_ref, g_ref, o_ref, gate_ref, x_ref,
                 glnx_ref, blnx_ref, rk_ref, lng_ref, lnb_ref, seg_ref, wob_ref, woa_ref, wout_ref,
                 out_ref, *, alpha):
    seg = seg_ref[...]
    d_model = x_ref.shape[1]
    inv_n = 1.0 / N_R
    y = y_ref[...]
    dy = y - _head_sum(y, seg) * inv_n
    var = _head_sum(dy * dy, seg) * inv_n
    yn = dy * lax.rsqrt(var + GN_EPS) * glnx_ref[...] + blnx_ref[...]
    bonus = _head_sum(r_ref[...] * k_ref[...] * rk_ref[...], seg) * v_ref[...]
    yb_in = ((yn + bonus) * g_ref[...]).astype(jnp.bfloat16)
    y_b = jnp.dot(yb_in, wob_ref[...], preferred_element_type=jnp.float32)
    y_a = jnp.dot(o_ref[...], woa_ref[...], preferred_element_type=jnp.float32)
    gates = gate_ref[...]
    z = gates[:, :d_model] * y_a + gates[:, d_model:] * y_b
    t = alpha * x_ref[...] + jnp.dot(z.astype(jnp.bfloat16), wout_ref[...],
                                     preferred_element_type=jnp.float32)
    mu = jnp.mean(t, axis=-1, keepdims=True)
    dt = t - mu
    vt = jnp.mean(dt * dt, axis=-1, keepdims=True)
    out_ref[...] = dt * lax.rsqrt(vt + LN_EPS) * lng_ref[...] + lnb_ref[...]


def _mixer_post(y, r, k, v, g, o, gates, x, g_lnx, b_lnx, r_k, ln_g, ln_b, w_ob, w_oa, w_out, alpha, v_col=0):
    m, d = x.shape
    tm = min(256, m)
    hl = LANES // 2
    seg = (jnp.arange(LANES)[:, None] // hl == jnp.arange(LANES)[None, :] // hl).astype(jnp.bfloat16)
    row = lambda n, j=0: pl.BlockSpec((tm, n), lambda i: (i, j))
    vec = pl.BlockSpec((1, d), lambda i: (0, 0))
    full = lambda a, b_: pl.BlockSpec((a, b_), lambda i: (0, 0))
    bf = jnp.bfloat16
    return pl.pallas_call(
        functools.partial(_post_kernel, alpha=alpha),
        grid=(m // tm,),
        in_specs=[row(d), row(d), row(d), row(d, v_col), row(d), row(d), row(2 * d), row(d),
                  vec, vec, vec, vec, vec, full(LANES, LANES), full(d, d), full(d, d), full(d, d)],
        out_specs=row(d),
        out_shape=jax.ShapeDtypeStruct((m, d), jnp.float32),
        compiler_params=pltpu.CompilerParams(
            dimension_semantics=("parallel",),
            vmem_limit_bytes=VMEM_LIMIT_BYTES),
        name="mixer_post",
    )(y, r, k, v, g, o, gates, x,
      g_lnx.reshape(1, d), b_lnx.reshape(1, d), r_k.reshape(1, d), ln_g.reshape(1, d), ln_b.reshape(1, d),
      seg, w_ob.astype(bf), w_oa.astype(bf), w_out.astype(bf))


def kernel(x, positions, w_in0, w_in_rest, b_gate, g_cq, w_uq, w_iq, g_ik, b_ik, w_oa, mu_rwkv, mu_vres, w0, w2, a0, a2, v0, v2, g2, k_k, k_a, r_k, g_lnx, b_lnx, w_ob, w_out, ln1_g, ln1_b, w_up, conv_w, conv_b, w_down, ln2_g, ln2_b):
    bsz, seq, d = x.shape
    m = bsz * seq
    alpha = (2 * DEPTH) ** 0.25
    topk = min(TOPK_MAX, seq // 4)
    tabs_a, tabs_i = _rope_tables_all(positions)
    v_first = None

    for i in range(DEPTH):
        w_in = w_in0 if i == 0 else w_in_rest[i - 1]
        zcols = lambda n: jnp.zeros((d, n), w_in.dtype)
        w_att = jnp.concatenate([w_in[:, :N_ATT], zcols(N_ATT_PAD - N_ATT)], axis=1)
        n_rw = w_in.shape[1] - (N_ATT + N_GATE)
        w_rw = jnp.concatenate([w_in[:, N_ATT + N_GATE:], zcols(N_RW_PAD - n_rw)], axis=1)
        mu_parts = [mu_rwkv[i]] + ([mu_vres[i - 1]] if i > 0 else [])
        mu_rw = jnp.concatenate(mu_parts + [jnp.zeros((N_RW_PAD - n_rw,), jnp.float32)])
        gates = _mm_gate(x.reshape(m, d), w_in[:, N_ATT:N_ATT + N_GATE], b_gate[i]).reshape(bsz, seq, N_GATE)
        p_rw = _mm_shift(x, w_rw, mu_rw)

        o = _attention(*_att_prep(x, w_att, g_cq[i], w_uq[i], w_iq[i], g_ik[i], b_ik[i], tabs_a, tabs_i), topk)

        f2 = lambda t_: t_.reshape(m, t_.shape[-1])
        f3 = lambda t_: t_.reshape(bsz, seq, t_.shape[-1])
        v_col = 2
        if i == 0:
            v_first = p_rw
            lw, k_r, kk_n, b_r, g_r = _rw_prep(f2(p_rw), None, w0[i], a0[i], None, k_k[i], k_a[i],
                                               w2[i], a2[i], g2[i], None)
            v_r = p_rw
        else:
            lw, k_r, kk_n, b_r, g_r, v_r = _rw_prep(f2(p_rw), f2(v_first), w0[i], a0[i], v0[i - 1],
                                                    k_k[i], k_a[i], w2[i], a2[i], g2[i], v2[i - 1])
            v_r = f3(v_r)
            v_col = 0
        y = _rwkv_scan(p_rw, f3(lw), f3(k_r), v_r, f3(kk_n), f3(b_r), r_col=0, v_col=v_col)

        x = _mixer_post(f2(y), f2(p_rw), k_r, f2(v_r), g_r, f2(o), f2(gates), f2(x),
                        g_lnx[i], b_lnx[i], r_k[i], ln1_g[i], ln1_b[i],
                        w_ob[i], w_oa[i], w_out[i], alpha, v_col=v_col).reshape(bsz, seq, d)

        x = _ffn(x, w_up[i], conv_w[i], conv_b[i], w_down[i], ln2_g[i], ln2_b[i], alpha)
    return x
```

```python
import functools

import jax
import jax.numpy as jnp
import numpy as np
from jax import lax
from jax.experimental import pallas as pl
from jax.experimental.pallas import tpu as pltpu

D_MODEL = 1024
DEPTH = 4
H_A = 8
DH_A = 128
D_CQ = 256
ROT_A = DH_A // 4
H_IDX = 8
D_IDX = 64
ROT_IDX = D_IDX // 4
TOPK_MAX = 256
ROPE_THETA = 500000.0
N_R = 64
H_R = D_MODEL // N_R
D_R = H_R * N_R
LORA_W = 64
LORA_A = 64
LORA_V = 32
LORA_G = 128
GN_EPS = 64e-5
D_FF = ((8 * D_MODEL // 3 + 127) // 128) * 128
LN_EPS = 1e-5

N_ATT = D_CQ + 2 * DH_A + D_IDX + H_IDX
N_GATE = 2 * D_MODEL
N_RWKV = 3 * D_R + LORA_W + LORA_A + LORA_G
N_IN0 = N_ATT + N_GATE + N_RWKV

LANES = 128
SUBLANES = 8
LOG2E = 1.4426950408889634
VMEM_LIMIT_BYTES = 48 * 1024 * 1024

N_ATT_PAD = 640
N_RW_PAD = 3584

ATTN_TILE = 256
SCAN_CHUNK = 64
SCAN_GROUP_TILES = 1
NEG_BIG = -1e30
INT_MIN = -(2 ** 31)


def _pick_tile(n, cap):
    best = None
    for t in range(LANES, min(n, cap) + 1, LANES):
        if n % t == 0:
            best = t
    return best if best is not None else n


def _gate_kernel(x_ref, w_ref, b_ref, o_ref):
    p = jnp.dot(x_ref[...].astype(jnp.bfloat16), w_ref[...], preferred_element_type=jnp.float32)
    o_ref[...] = jax.nn.sigmoid(p + b_ref[...])


def _mm_gate(x, w, bias):
    m, k = x.shape
    n = w.shape[1]
    tm = min(1024, m)
    tn = _pick_tile(n, 1024)
    return pl.pallas_call(
        _gate_kernel,
        grid=(m // tm, n // tn),
        in_specs=[pl.BlockSpec((tm, k), lambda i, j: (i, 0)),
                  pl.BlockSpec((k, tn), lambda i, j: (0, j)),
                  pl.BlockSpec((1, tn), lambda i, j: (0, j))],
        out_specs=pl.BlockSpec((tm, tn), lambda i, j: (i, j)),
        out_shape=jax.ShapeDtypeStruct((m, n), jnp.float32),
        compiler_params=pltpu.CompilerParams(
            dimension_semantics=("parallel", "arbitrary"),
            vmem_limit_bytes=VMEM_LIMIT_BYTES),
        name="mm_gate",
    )(x, w.astype(jnp.bfloat16), bias.reshape(1, n))


def _shift_rows(cur, prev_rows, k):
    rolled = pltpu.roll(cur, k, 0)
    row = lax.broadcasted_iota(jnp.int32, cur.shape, 0)
    for i in range(k):
        rolled = jnp.where(row == i, prev_rows[i:i + 1, :], rolled)
    return rolled


def _shift_mm_kernel(x_ref, w_ref, mu_ref, o_ref, carry_ref):
    si = pl.program_id(1)
    j = pl.program_id(2)
    tm = o_ref.shape[1]
    p = jnp.dot(x_ref[0].astype(jnp.bfloat16), w_ref[...], preferred_element_type=jnp.float32)
    @pl.when(si == 0)
    def _():
        carry_ref[j] = jnp.zeros(carry_ref.shape[1:], jnp.float32)

    prev = _shift_rows(p, carry_ref[j, 0:1, :], 1)
    carry_ref[j, 0:1, :] = p[tm - 1:tm, :]
    o_ref[0] = p + (prev - p) * mu_ref[...]


def _mm_shift(x, w, mu):
    b, s, k = x.shape
    n = w.shape[1]
    tm = min(1024, s)
    tn = 512 if n % 512 == 0 else _pick_tile(n, 1024)
    return pl.pallas_call(
        _shift_mm_kernel,
        grid=(b, s // tm, n // tn),
        in_specs=[pl.BlockSpec((1, tm, k), lambda bi, si, j: (bi, si, 0)),
                  pl.BlockSpec((k, tn), lambda bi, si, j: (0, j)),
                  pl.BlockSpec((1, tn), lambda bi, si, j: (0, j))],
        out_specs=pl.BlockSpec((1, tm, tn), lambda bi, si, j: (bi, si, j)),
        out_shape=jax.ShapeDtypeStruct((b, s, n), jnp.float32),
        scratch_shapes=[pltpu.VMEM((n // tn, 8, tn), jnp.float32)],
        compiler_params=pltpu.CompilerParams(
            dimension_semantics=("parallel", "arbitrary", "arbitrary"),
            vmem_limit_bytes=VMEM_LIMIT_BYTES),
        name="mm_shift",
    )(x, w.astype(jnp.bfloat16), mu.reshape(1, n))


def _ffn_kernel(x_ref, wg_ref, wv_ref, cw_ref, cb_ref, wd_ref, g_ref, b_ref, o_ref,
                xb_ref, acc_ref, carry_ref, *, alpha, eps):
    si = pl.program_id(1)
    c = pl.program_id(2)
    tm = o_ref.shape[1]

    @pl.when(c == 0)
    def _():
        xb_ref[...] = x_ref[0].astype(jnp.bfloat16)
        acc_ref[...] = jnp.zeros(acc_ref.shape, jnp.float32)

    @pl.when(si == 0)
    def _():
        carry_ref[c] = jnp.zeros(carry_ref.shape[1:], jnp.float32)

    nsub = 2
    rows = tm // nsub
    hg = [jnp.dot(xb_ref[i * rows:(i + 1) * rows, :], wg_ref[...], preferred_element_type=jnp.float32)
          for i in range(nsub)]
    hv = [jnp.dot(xb_ref[i * rows:(i + 1) * rows, :], wv_ref[...], preferred_element_type=jnp.float32)
          for i in range(nsub)]
    tail = carry_ref[c, 0:2, :]
    carry_ref[c, 0:2, :] = hg[nsub - 1][rows - 2:rows, :]
    cw = cw_ref[...]
    for i in range(nsub):
        g1 = _shift_rows(hg[i], tail[1:2, :], 1)
        g2 = _shift_rows(hg[i], tail, 2)
        tail = hg[i][rows - 2:rows, :]
        conv = g2 * cw[0:1, :] + g1 * cw[1:2, :] + hg[i] * cw[2:3, :] + cb_ref[...]
        act = jax.nn.silu(conv) * hv[i]
        acc_ref[i * rows:(i + 1) * rows, :] += jnp.dot(act.astype(jnp.bfloat16), wd_ref[...],
                                                       preferred_element_type=jnp.float32)

    @pl.when(c == pl.num_programs(2) - 1)
    def _():
        y = alpha * x_ref[0] + acc_ref[...]
        mu = jnp.mean(y, axis=-1, keepdims=True)
        d = y - mu
        var = jnp.mean(d * d, axis=-1, keepdims=True)
        o_ref[0] = d * lax.rsqrt(var + eps) * g_ref[...] + b_ref[...]


def _ffn(x, w_up, conv_w, conv_b, w_down, ln_g, ln_b, alpha):
    b, s, d = x.shape
    dff = w_down.shape[0]
    tm = min(1024, s)
    tf = 256 if dff % 256 == 0 else LANES
    nc = dff // tf
    kern = functools.partial(_ffn_kernel, alpha=alpha, eps=LN_EPS)
    w_up = w_up.astype(jnp.bfloat16)
    return pl.pallas_call(
        kern,
        grid=(b, s // tm, nc),
        in_specs=[
            pl.BlockSpec((1, tm, d), lambda bi, si, c: (bi, si, 0)),
            pl.BlockSpec((d, tf), lambda bi, si, c: (0, c)),
            pl.BlockSpec((d, tf), lambda bi, si, c: (0, c + nc)),
            pl.BlockSpec((3, tf), lambda bi, si, c: (0, c)),
            pl.BlockSpec((1, tf), lambda bi, si, c: (0, c)),
            pl.BlockSpec((tf, d), lambda bi, si, c: (c, 0)),
            pl.BlockSpec((1, d), lambda bi, si, c: (0, 0)),
            pl.BlockSpec((1, d), lambda bi, si, c: (0, 0)),
        ],
        out_specs=pl.BlockSpec((1, tm, d), lambda bi, si, c: (bi, si, 0)),
        out_shape=jax.ShapeDtypeStruct((b, s, d), jnp.float32),
        scratch_shapes=[
            pltpu.VMEM((tm, d), jnp.bfloat16),
            pltpu.VMEM((tm, d), jnp.float32),
            pltpu.VMEM((nc, 8, tf), jnp.float32),
        ],
        compiler_params=pltpu.CompilerParams(
            dimension_semantics=("parallel", "arbitrary", "arbitrary"),
            vmem_limit_bytes=VMEM_LIMIT_BYTES),
        name="ffn",
    )(x, w_up, w_up, conv_w, conv_b.reshape(1, dff), w_down.astype(jnp.bfloat16),
      ln_g.reshape(1, d), ln_b.reshape(1, d))


def _sortable_key(score):
    score = jnp.where(score == 0.0, 0.0, score)
    bits = pltpu.bitcast(score, jnp.int32)
    return jnp.where(bits < 0, bits ^ jnp.int32(0x7FFFFFFF), bits)


def _attn_kernel(qt_ref, qit_ref, wi_ref, k_ref, vt_ref, ki_ref, o_ref,
                 key_ref, jthr_ref, m_ref, acc_ref, *, t, nh, topk):
    qb = pl.program_id(1)
    n_kt = qb + 1
    krow = lax.broadcasted_iota(jnp.int32, (t, t), 0)
    qcol = lax.broadcasted_iota(jnp.int32, (t, t), 1)

    qit = qit_ref[0, 0]
    wi = wi_ref[0, 0]

    def score_tile(kt, carry):
        ki_t = ki_ref[0, pl.ds(kt * t, t), :]
        rel = jnp.dot(ki_t, qit, preferred_element_type=jnp.float32)
        rel = jnp.maximum(rel, 0.0) * wi
        score = rel[:, 0:t]
        for h in range(1, nh):
            score = score + rel[:, h * t:(h + 1) * t]
        score = jnp.where(jnp.logical_and(kt == qb, krow > qcol), -jnp.inf, score)
        key_ref[kt] = _sortable_key(score)
        return carry

    lax.fori_loop(0, n_kt, score_tile, 0)

    def count(pred_fn):
        def body(kt, accs):
            hit = pred_fn(key_ref[kt], kt)
            accs = list(accs)
            for j in range(t // 8):
                a = accs[j % len(accs)]
                accs[j % len(accs)] = jnp.where(hit[j * 8:(j + 1) * 8], a + 1, a)
            return tuple(accs)
        accs = lax.fori_loop(0, n_kt, body, tuple(jnp.zeros((8, t), jnp.int32) for _ in range(4)))
        return jnp.sum(accs[0] + accs[1] + accs[2] + accs[3], axis=0, keepdims=True)

    def count_ge(cand):
        return count(lambda kk, kt: kk >= cand)

    cnt0 = count_ge(jnp.zeros((1, t), jnp.int32))
    nonneg = cnt0 >= topk
    thr0 = jnp.where(nonneg, jnp.int32(0), jnp.int32(INT_MIN))
    n_ge0 = jnp.where(nonneg, cnt0, n_kt * t)

    def bit_step(i, state):
        thr, n_ge = state
        cand = thr | jnp.left_shift(jnp.int32(1), 30 - i)
        cnt = count_ge(cand)
        take = cnt >= topk
        return jnp.where(take, cand, thr), jnp.where(take, cnt, n_ge)

    thr, n_ge = lax.fori_loop(0, 31, bit_step, (thr0, n_ge0))
    jthr_ref[...] = jnp.full(jthr_ref.shape, 2 ** 30, jnp.int32)

    @pl.when(jnp.max(n_ge) > topk)
    def _():
        need = topk - count(lambda kk, kt: kk > thr)
        idx_bits = int(np.ceil(np.log2(key_ref.shape[0] * t)))

        def idx_step(i, j):
            cand = j | jnp.left_shift(jnp.int32(1), idx_bits - 1 - i)
            below = count(lambda kk, kt: jnp.logical_and(kk == thr, krow + kt * t < cand))
            return jnp.where(below < need, cand, j)

        jthr_ref[...] = lax.fori_loop(0, idx_bits, idx_step, jnp.zeros((1, t), jnp.int32))

    qt = qt_ref[0, 0]
    m_ref[...] = jnp.full(m_ref.shape, NEG_BIG, jnp.float32)
    acc_ref[...] = jnp.zeros(acc_ref.shape, jnp.float32)
    jthr = jthr_ref[...]

    def attn_tile(kt, carry):
        kk = key_ref[kt]
        gk = krow + kt * t
        sel = jnp.logical_or(kk > thr, jnp.logical_and(kk == thr, gk <= jthr))
        sel = jnp.logical_and(sel, gk <= qcol + qb * t)
        bias = jnp.where(sel, 0.0, NEG_BIG)
        k_t = k_ref[0, pl.ds(kt * t, t), :]
        vt_t = vt_ref[0, kt]
        s_all = jnp.dot(k_t, qt, preferred_element_type=jnp.float32)
        for h in range(nh):
            s = s_all[:, h * t:(h + 1) * t] + bias
            m_old = m_ref[h:h + 1, :]
            m_new = jnp.maximum(m_old, jnp.max(s, axis=0, keepdims=True))
            alpha = jnp.exp2(m_old - m_new)
            p = jnp.exp2(s - m_new)
            acc_ref[h] = acc_ref[h] * alpha + jnp.dot(
                vt_t, p.astype(jnp.bfloat16), preferred_element_type=jnp.float32)
            m_ref[h:h + 1, :] = m_new
        return carry

    lax.fori_loop(0, n_kt, attn_tile, 0)
    dh = o_ref.shape[2] // nh
    for h in range(nh):
        acc = acc_ref[h]
        out = acc[:dh] / acc[dh:dh + 1]
        o_ref[0, :, h * dh:(h + 1) * dh] = out.T.astype(o_ref.dtype)


def _rope_rows(blk, cos, sin, half):
    x1, x2 = blk[:half], blk[half:2 * half]
    return jnp.concatenate([x1 * cos - x2 * sin, x2 * cos + x1 * sin, blk[2 * half:]], axis=0)


def _rope_lanes(x, c, sa, sb, half):
    return x * c + pltpu.roll(x, LANES - half, 1) * sa + pltpu.roll(x, half, 1) * sb


def _attprep_kernel(x_ref, watt_ref, gcq_ref, wuqt_ref, wiqt_ref, gik_ref, bik_ref,
                    cosa_ref, sina_ref, cosi_ref, sini_ref,
                    kc_ref, ksa_ref, ksb_ref, ic_ref, isa_ref, isb_ref,
                    qt_ref, qit_ref, wit_ref, k_ref, vt_ref, ki_ref, *, t, nh, q_scale, w_scale):
    bf = jnp.bfloat16
    nt = (((1,), (1,)), ((), ()))
    p = jnp.dot(x_ref[0].astype(bf), watt_ref[...], preferred_element_type=jnp.float32)
    cq = p[:, :D_CQ]
    c_q = cq * lax.rsqrt(jnp.mean(cq * cq, axis=-1, keepdims=True) + 1e-6) * gcq_ref[...]
    cqb = c_q.astype(bf)

    q_t = lax.dot_general(wuqt_ref[...], cqb, nt, preferred_element_type=jnp.float32)
    cos, sin = cosa_ref[0], sina_ref[0]
    for h in range(nh):
        blk = _rope_rows(q_t[h * DH_A:(h + 1) * DH_A], cos, sin, ROT_A // 2) * q_scale
        qt_ref[0, 0, :, h * t:(h + 1) * t] = blk.astype(bf)
    qi_t = lax.dot_general(wiqt_ref[...], cqb, nt, preferred_element_type=jnp.float32)
    cos, sin = cosi_ref[0], sini_ref[0]
    for h in range(nh):
        blk = _rope_rows(qi_t[h * D_IDX:(h + 1) * D_IDX], cos, sin, ROT_IDX // 2)
        qit_ref[0, 0, :, h * t:(h + 1) * t] = blk.astype(bf)

    k_raw = p[:, D_CQ:D_CQ + DH_A]
    k_ref[0] = _rope_lanes(k_raw, kc_ref[0], ksa_ref[0], ksb_ref[0], ROT_A // 2).astype(bf)
    v_raw = p[:, D_CQ + DH_A:D_CQ + 2 * DH_A]
    vt_ref[0, 0, :DH_A, :] = v_raw.T.astype(bf)
    vt_ref[0, 0, DH_A:, :] = jnp.ones((vt_ref.shape[2] - DH_A, t), bf)

    tail = p[:, D_CQ + 2 * DH_A:]
    lane = lax.broadcasted_iota(jnp.int32, (1, LANES), 1)
    is_key = lane < D_IDX
    mu = jnp.sum(jnp.where(is_key, tail, 0.0), axis=-1, keepdims=True) * (1.0 / D_IDX)
    dk = jnp.where(is_key, tail - mu, 0.0)
    var = jnp.sum(dk * dk, axis=-1, keepdims=True) * (1.0 / D_IDX)
    ln = dk * lax.rsqrt(var + LN_EPS) * gik_ref[...] + bik_ref[...]
    ki = _rope_lanes(ln, ic_ref[0], isa_ref[0], isb_ref[0], ROT_IDX // 2)
    ki_ref[0] = ki[:, :D_IDX].astype(bf)
    w_t = tail.T[D_IDX:D_IDX + nh] * w_scale
    for h in range(nh):
        wit_ref[0, 0, :, h * t:(h + 1) * t] = w_t[h:h + 1]


def _rope_tables_all(positions):
    def tables(rot):
        half = rot // 2
        inv_freq = jnp.power(jnp.float32(ROPE_THETA), -jnp.arange(0, rot, 2, dtype=jnp.float32) / rot)
        ang = positions.astype(jnp.float32)[..., None] * inv_freq
        cos, sin = jnp.cos(ang), jnp.sin(ang)
        ones = jnp.ones(ang.shape[:2] + (LANES - 2 * half,), jnp.float32)
        zeros = lambda n: jnp.zeros(ang.shape[:2] + (n,), jnp.float32)
        c = jnp.concatenate([cos, cos, ones], axis=-1)
        sa = jnp.concatenate([-sin, zeros(LANES - half)], axis=-1)
        sb = jnp.concatenate([zeros(half), sin, zeros(LANES - 2 * half)], axis=-1)
        return cos.swapaxes(1, 2), sin.swapaxes(1, 2), c, sa, sb
    return tables(ROT_A), tables(ROT_IDX)


def _att_prep(x, w_att, g_cq, w_uq, w_iq, g_ik, b_ik, tabs_a, tabs_i):
    b, s, d = x.shape
    t = min(ATTN_TILE, s)
    nq = s // t
    nh = H_A
    bf = jnp.bfloat16
    dhe = DH_A + SUBLANES
    pad_vec = lambda vv: jnp.concatenate([vv, jnp.zeros((LANES - vv.shape[0],), vv.dtype)]).reshape(1, LANES)
    cos_a, sin_a, kc, ksa, ksb = tabs_a
    cos_i, sin_i, ic, isa, isb = tabs_i
    const = lambda a, b_: pl.BlockSpec((a, b_), lambda bi, qb: (0, 0))
    fm = lambda rows: pl.BlockSpec((1, rows, t), lambda bi, qb: (bi, 0, qb))
    tm_ = pl.BlockSpec((1, t, LANES), lambda bi, qb: (bi, qb, 0))
    kern = functools.partial(_attprep_kernel, t=t, nh=nh,
                             q_scale=float(DH_A) ** -0.5 * LOG2E, w_scale=H_IDX ** -0.5 * D_IDX ** -0.5)
    return pl.pallas_call(
        kern,
        grid=(b, nq),
        in_specs=[
            pl.BlockSpec((1, t, d), lambda bi, qb: (bi, qb, 0)),
            const(d, N_ATT_PAD), const(1, D_CQ), const(nh * DH_A, D_CQ), const(nh * D_IDX, D_CQ),
            const(1, LANES), const(1, LANES),
            fm(ROT_A // 2), fm(ROT_A // 2), fm(ROT_IDX // 2), fm(ROT_IDX // 2),
            tm_, tm_, tm_, tm_, tm_, tm_,
        ],
        out_specs=[
            pl.BlockSpec((1, 1, DH_A, nh * t), lambda bi, qb: (bi, qb, 0, 0)),
            pl.BlockSpec((1, 1, D_IDX, nh * t), lambda bi, qb: (bi, qb, 0, 0)),
            pl.BlockSpec((1, 1, 1, nh * t), lambda bi, qb: (bi, qb, 0, 0)),
            pl.BlockSpec((1, t, DH_A), lambda bi, qb: (bi, qb, 0)),
            pl.BlockSpec((1, 1, dhe, t), lambda bi, qb: (bi, qb, 0, 0)),
            pl.BlockSpec((1, t, D_IDX), lambda bi, qb: (bi, qb, 0)),
        ],
        out_shape=[
            jax.ShapeDtypeStruct((b, nq, DH_A, nh * t), bf),
            jax.ShapeDtypeStruct((b, nq, D_IDX, nh * t), bf),
            jax.ShapeDtypeStruct((b, nq, 1, nh * t), jnp.float32),
            jax.ShapeDtypeStruct((b, s, DH_A), bf),
            jax.ShapeDtypeStruct((b, nq, dhe, t), bf),
            jax.ShapeDtypeStruct((b, s, D_IDX), bf),
        ],
        compiler_params=pltpu.CompilerParams(
            dimension_semantics=("parallel", "arbitrary"),
            vmem_limit_bytes=VMEM_LIMIT_BYTES),
        name="att_prep",
    )(x, w_att.astype(bf), g_cq.reshape(1, D_CQ), w_uq.T.astype(bf), w_iq.T.astype(bf),
      pad_vec(g_ik), pad_vec(b_ik), cos_a, sin_a, cos_i, sin_i, kc, ksa, ksb, ic, isa, isb)


def _attention(qt, qit, wit, k, vt, ki, topk):
    b, nq, dh, nht = qt.shape
    di = qit.shape[2]
    s = k.shape[1]
    t = s // nq
    nh = nht // t
    dhe = vt.shape[2]
    kern = functools.partial(_attn_kernel, t=t, nh=nh, topk=topk)
    return pl.pallas_call(
        kern,
        grid=(b, nq),
        in_specs=[
            pl.BlockSpec((1, 1, dh, nh * t), lambda bi, qb: (bi, qb, 0, 0)),
            pl.BlockSpec((1, 1, di, nh * t), lambda bi, qb: (bi, qb, 0, 0)),
            pl.BlockSpec((1, 1, 1, nh * t), lambda bi, qb: (bi, qb, 0, 0)),
            pl.BlockSpec((1, s, dh), lambda bi, qb: (bi, 0, 0)),
            pl.BlockSpec((1, nq, dhe, t), lambda bi, qb: (bi, 0, 0, 0)),
            pl.BlockSpec((1, s, di), lambda bi, qb: (bi, 0, 0)),
        ],
        out_specs=pl.BlockSpec((1, t, nh * dh), lambda bi, qb: (bi, qb, 0)),
        out_shape=jax.ShapeDtypeStruct((b, s, nh * dh), jnp.bfloat16),
        scratch_shapes=[
            pltpu.VMEM((nq, t, t), jnp.int32),
            pltpu.VMEM((1, t), jnp.int32),
            pltpu.VMEM((nh, t), jnp.float32),
            pltpu.VMEM((nh, dhe, t), jnp.float32),
        ],
        compiler_params=pltpu.CompilerParams(
            dimension_semantics=("parallel", "arbitrary"),
            vmem_limit_bytes=VMEM_LIMIT_BYTES),
        name="dsa_attention",
    )(qt, qit, wit, k, vt, ki)


def _bdot(a, b, dims=(((1,), (0,)), ((), ()))):
    return lax.dot_general(a.astype(jnp.bfloat16), b.astype(jnp.bfloat16), dims,
                           preferred_element_type=jnp.float32)


_NT = (((1,), (1,)), ((), ()))
_TN = (((0,), (0,)), ((), ()))


def _scan_kernel(r_ref, lw_ref, k_ref, v_ref, kk_ref, b_ref, y_ref, s_ref, *, chunk):
    @pl.when(pl.program_id(1) == 0)
    def _():
        s_ref[...] = jnp.zeros(s_ref.shape, jnp.float32)

    n_tiles = s_ref.shape[0]
    hl = LANES // 2
    gt = SCAN_GROUP_TILES
    rows = 2 * gt * chunk
    row = lax.broadcasted_iota(jnp.int32, (rows, rows), 0)
    col = lax.broadcasted_iota(jnp.int32, (rows, rows), 1)
    same = (row // chunk) == (col // chunk)
    incl = jnp.logical_and(same, row >= col)
    strict = jnp.logical_and(same, row > col)
    r1 = lax.broadcasted_iota(jnp.int32, (chunk, chunk), 0)
    c1 = lax.broadcasted_iota(jnp.int32, (chunk, chunk), 1)
    tril = jnp.where(r1 >= c1, 1.0, 0.0).astype(jnp.float32)
    lane = lax.broadcasted_iota(jnp.int32, (1, LANES), 1)
    even = lane < hl
    pr_ = lax.broadcasted_iota(jnp.int32, (LANES, LANES), 0)
    pc_ = lax.broadcasted_iota(jnp.int32, (LANES, LANES), 1)
    pair_bd = (pr_ // hl) == (pc_ // hl)

    lw = lw_ref[0]
    c = jnp.dot(tril, lw, precision=lax.Precision.HIGHEST,
                preferred_element_type=jnp.float32)
    e_neg = jnp.exp(-c)
    at = -kk_ref[0] * jnp.exp(c - lw)
    rt = r_ref[0] * jnp.exp(c)
    bt = b_ref[0] * e_neg
    kt = k_ref[0] * e_neg
    v = v_ref[0]
    p_last = jnp.exp(c[chunk - 1:chunk, :])
    bp = bt * p_last
    kp = kt * p_last

    def tile(x, t):
        return x[:, t * LANES:(t + 1) * LANES]

    def stack_masked(x, t0):
        parts = []
        for t in range(t0, t0 + gt):
            xt = tile(x, t)
            parts += [jnp.where(even, xt, 0.0), jnp.where(even, 0.0, xt)]
        return jnp.concatenate(parts, axis=0)

    def stack_plain(x, t0):
        return jnp.concatenate([tile(x, t) for t in range(t0, t0 + gt) for _ in range(2)], axis=0)

    steps = max(1, int(np.ceil(np.log2(chunk))))
    groups = list(range(0, n_tiles, gt))
    at_s = [stack_masked(at, t0) for t0 in groups]
    v_s = [stack_masked(v, t0) for t0 in groups]
    a_all = [_bdot(jnp.concatenate([at_s[g], stack_masked(rt, t0)], axis=0),
                   jnp.concatenate([stack_plain(bt, t0), stack_plain(kt, t0)], axis=0), _NT)
             for g, t0 in enumerate(groups)]
    mpow = [jnp.where(strict, a[:rows, :rows], 0.0) for a in a_all]
    a_ak = [jnp.where(strict, a[:rows, rows:], 0.0) for a in a_all]
    a_rb = [jnp.where(incl, a[rows:, :rows], 0.0) for a in a_all]
    a_rk = [jnp.where(incl, a[rows:, rows:], 0.0) for a in a_all]

    xs = [jnp.concatenate([at_s[g], _bdot(a_ak[g], v_s[g])], axis=1) for g in range(len(groups))]
    for i in range(steps):
        xs = [x + _bdot(m_, x) for x, m_ in zip(xs, mpow)]
        if i + 1 < steps:
            mpow = [_bdot(m_, m_) for m_ in mpow]
    qys = [_bdot(a_rb[g], xs[g]) for g in range(len(groups))]
    y0s = [qys[g][:, LANES:] + _bdot(a_rk[g], v_s[g]) for g in range(len(groups))]

    for g, t0 in enumerate(groups):
        x = xs[g]
        y0_s = y0s[g]
        qh_s = qys[g][:, :LANES]
        for j in range(gt):
            t = t0 + j
            lo, mid, hi = 2 * j * chunk, (2 * j + 1) * chunk, (2 * j + 2) * chunk
            unstack = lambda z: z[lo:mid] + z[mid:hi]
            x_t = unstack(x)
            qh_t = tile(rt, t) + unstack(qh_s)
            y0_t = unstack(y0_s)
            s0 = s_ref[t]
            gw = _bdot(x_t, tile(bp, t), _TN)
            g_t = jnp.where(pair_bd, gw[:LANES], 0.0)
            h_t = jnp.where(pair_bd, gw[LANES:] + _bdot(tile(v, t), tile(kp, t), _TN), 0.0)
            y_ref[0, :, t * LANES:(t + 1) * LANES] = y0_t + _bdot(qh_t, s0, _NT)
            s_ref[t] = s0 * tile(p_last, t) + _bdot(s0, g_t) + h_t


def _rwkv_scan(r, lw, k, v, kk, b, r_col=0, v_col=0):
    bsz, s, _ = lw.shape
    d = D_R
    chunk = min(SCAN_CHUNK, s)
    spec = pl.BlockSpec((1, chunk, d), lambda bi, ci: (bi, ci, 0))
    colspec = lambda j: pl.BlockSpec((1, chunk, d), lambda bi, ci, j=j: (bi, ci, j))
    kern = functools.partial(_scan_kernel, chunk=chunk)
    return pl.pallas_call(
        kern,
        grid=(bsz, s // chunk),
        in_specs=[colspec(r_col), spec, spec, colspec(v_col), spec, spec],
        out_specs=spec,
        out_shape=jax.ShapeDtypeStruct((bsz, s, d), jnp.float32),
        scratch_shapes=[pltpu.VMEM((d // LANES, LANES, LANES), jnp.float32)],
        compiler_params=pltpu.CompilerParams(
            dimension_semantics=("parallel", "arbitrary"),
            vmem_limit_bytes=VMEM_LIMIT_BYTES),
        name="rwkv7_scan",
    )(r, lw, k, v, kk, b)


RW_COL_LORA = 3 * D_R // LANES
RW_COL_GATE = RW_COL_LORA + 1
RW_COL_VRES = RW_COL_LORA + 2


def _rwprep_kernel(*refs, has_vres):
    if has_vres:
        (pk_ref, pv_ref, pwa_ref, pg_ref, pvr_ref, vf_ref, w0_ref, a0_ref, v0_ref, kk_ref, ka_ref,
         w2_ref, a2_ref, g2_ref, v2_ref, seg_ref, lw_ref, k_out, kkn_ref, b_ref, g_out, v_out) = refs
    else:
        (pk_ref, pwa_ref, pg_ref, w0_ref, a0_ref, kk_ref, ka_ref,
         w2_ref, a2_ref, g2_ref, seg_ref, lw_ref, k_out, kkn_ref, b_ref, g_out) = refs
    bf = jnp.bfloat16
    dotf = lambda a, w: jnp.dot(a.astype(bf), w, preferred_element_type=jnp.float32)
    pwa = pwa_ref[...]
    u = w0_ref[...] + dotf(jnp.tanh(pwa), w2_ref[...])
    neg = -u
    softplus = jnp.maximum(neg, 0.0) + jnp.log(1.0 + jnp.exp(-jnp.abs(neg)))
    lw_ref[...] = -jnp.exp(-softplus - 0.5)
    a = jax.nn.sigmoid(a0_ref[...] + dotf(pwa, a2_ref[...]))
    g_out[...] = dotf(jax.nn.sigmoid(pg_ref[...]), g2_ref[...])
    pk = pk_ref[...]
    kk = pk * kk_ref[...]
    norm = jnp.sqrt(_head_sum(kk * kk, seg_ref[...]))
    kk = kk / jnp.maximum(norm, 1e-12)
    kkn_ref[...] = kk
    b_ref[...] = kk * a
    k_out[...] = pk * (1.0 + (a - 1.0) * ka_ref[...])
    if has_vres:
        pv = pv_ref[...]
        mix = jax.nn.sigmoid(v0_ref[...] + dotf(pvr_ref[...], v2_ref[...]))
        v_out[...] = pv + (vf_ref[...] - pv) * mix


def _rw_prep(p_rw, v_first, w0, a0, v0, k_k, k_a, w2, a2, g2, v2):
    m = p_rw.shape[0]
    d = D_R
    tm = min(256, m)
    has_vres = v_first is not None
    bf = jnp.bfloat16
    hl = LANES // 2
    seg = (jnp.arange(LANES)[:, None] // hl == jnp.arange(LANES)[None, :] // hl).astype(bf)
    pad_rows = lambda w, top: jnp.concatenate(
        [jnp.zeros((top, d), w.dtype), w, jnp.zeros((LANES - top - w.shape[0], d), w.dtype)], axis=0).astype(bf)
    col = lambda j, width: pl.BlockSpec((tm, width), lambda i, j=j: (i, j))
    vec = pl.BlockSpec((1, d), lambda i: (0, 0))
    wspec = pl.BlockSpec((LANES, d), lambda i: (0, 0))
    segspec = pl.BlockSpec((LANES, LANES), lambda i: (0, 0))
    row_out = pl.BlockSpec((tm, d), lambda i: (i, 0))
    v1 = lambda t_: t_.reshape(1, d)
    if has_vres:
        args = [p_rw, p_rw, p_rw, p_rw, p_rw, v_first, v1(w0), v1(a0), v1(v0), v1(k_k), v1(k_a),
                pad_rows(w2, 0), pad_rows(a2, LORA_W), g2.astype(bf), pad_rows(v2, 0), seg]
        in_specs = [col(1, d), col(2, d), col(RW_COL_LORA, LANES), col(RW_COL_GATE, LANES),
                    col(RW_COL_VRES, LANES), col(2, d), vec, vec, vec, vec, vec,
                    wspec, wspec, wspec, wspec, segspec]
        n_out = 6
    else:
        args = [p_rw, p_rw, p_rw, v1(w0), v1(a0), v1(k_k), v1(k_a),
                pad_rows(w2, 0), pad_rows(a2, LORA_W), g2.astype(bf), seg]
        in_specs = [col(1, d), col(RW_COL_LORA, LANES), col(RW_COL_GATE, LANES), vec, vec, vec, vec,
                    wspec, wspec, wspec, segspec]
        n_out = 5
    return pl.pallas_call(
        functools.partial(_rwprep_kernel, has_vres=has_vres),
        grid=(m // tm,),
        in_specs=in_specs,
        out_specs=[row_out] * n_out,
        out_shape=[jax.ShapeDtypeStruct((m, d), jnp.float32)] * n_out,
        compiler_params=pltpu.CompilerParams(
            dimension_semantics=("parallel",),
            vmem_limit_bytes=VMEM_LIMIT_BYTES),
        name="rw_prep",
    )(*args)


def _head_sum(z, seg):
    hi = z.astype(jnp.bfloat16)
    lo = (z - hi.astype(jnp.float32)).astype(jnp.bfloat16)
    outs = []
    for t in range(z.shape[1] // LANES):
        sl = slice(t * LANES, (t + 1) * LANES)
        outs.append(jnp.dot(hi[:, sl], seg, preferred_element_type=jnp.float32)
                    + jnp.dot(lo[:, sl], seg, preferred_element_type=jnp.float32))
    return jnp.concatenate(outs, axis=1)


def _post_kernel(y_ref, r_ref, k_ref, v_ref, g_ref, o_ref, gate_ref, x_ref,
                 glnx_ref, blnx_ref, rk_ref, lng_ref, lnb_ref, seg_ref, wob_ref, woa_ref, wout_ref,
                 out_ref, *, alpha):
    seg = seg_ref[...]
    d_model = x_ref.shape[1]
    inv_n = 1.0 / N_R
    y = y_ref[...]
    dy = y - _head_sum(y, seg) * inv_n
    var = _head_sum(dy * dy, seg) * inv_n
    yn = dy * lax.rsqrt(var + GN_EPS) * glnx_ref[...] + blnx_ref[...]
    bonus = _head_sum(r_ref[...] * k_ref[...] * rk_ref[...], seg) * v_ref[...]
    yb_in = ((yn + bonus) * g_ref[...]).astype(jnp.bfloat16)
    y_b = jnp.dot(yb_in, wob_ref[...], preferred_element_type=jnp.float32)
    y_a = jnp.dot(o_ref[...], woa_ref[...], preferred_element_type=jnp.float32)
    gates = gate_ref[...]
    z = gates[:, :d_model] * y_a + gates[:, d_model:] * y_b
    t = alpha * x_ref[...] + jnp.dot(z.astype(jnp.bfloat16), wout_ref[...],
                                     preferred_element_type=jnp.float32)
    mu = jnp.mean(t, axis=-1, keepdims=True)
    dt = t - mu
    vt = jnp.mean(dt * dt, axis=-1, keepdims=True)
    out_ref[...] = dt * lax.rsqrt(vt + LN_EPS) * lng_ref[...] + lnb_ref[...]


def _mixer_post(y, r, k, v, g, o, gates, x, g_lnx, b_lnx, r_k, ln_g, ln_b, w_ob, w_oa, w_out, alpha, v_col=0):
    m, d = x.shape
    tm = min(256, m)
    hl = LANES // 2
    seg = (jnp.arange(LANES)[:, None] // hl == jnp.arange(LANES)[None, :] // hl).astype(jnp.bfloat16)
    row = lambda n, j=0: pl.BlockSpec((tm, n), lambda i: (i, j))
    vec = pl.BlockSpec((1, d), lambda i: (0, 0))
    full = lambda a, b_: pl.BlockSpec((a, b_), lambda i: (0, 0))
    bf = jnp.bfloat16
    return pl.pallas_call(
        functools.partial(_post_kernel, alpha=alpha),
        grid=(m // tm,),
        in_specs=[row(d), row(d), row(d), row(d, v_col), row(d), row(d), row(2 * d), row(d),
                  vec, vec, vec, vec, vec, full(LANES, LANES), full(d, d), full(d, d), full(d, d)],
        out_specs=row(d),
        out_shape=jax.ShapeDtypeStruct((m, d), jnp.float32),
        compiler_params=pltpu.CompilerParams(
            dimension_semantics=("parallel",),
            vmem_limit_bytes=VMEM_LIMIT_BYTES),
        name="mixer_post",
    )(y, r, k, v, g, o, gates, x,
      g_lnx.reshape(1, d), b_lnx.reshape(1, d), r_k.reshape(1, d), ln_g.reshape(1, d), ln_b.reshape(1, d),
      seg, w_ob.astype(bf), w_oa.astype(bf), w_out.astype(bf))


def kernel(x, positions, w_in0, w_in_rest, b_gate, g_cq, w_uq, w_iq, g_ik, b_ik, w_oa, mu_rwkv, mu_vres, w0, w2, a0, a2, v0, v2, g2, k_k, k_a, r_k, g_lnx, b_lnx, w_ob, w_out, ln1_g, ln1_b, w_up, conv_w, conv_b, w_down, ln2_g, ln2_b):
    bsz, seq, d = x.shape
    m = bsz * seq
    alpha = (2 * DEPTH) ** 0.25
    topk = min(TOPK_MAX, seq // 4)
    tabs_a, tabs_i = _rope_tables_all(positions)
    v_first = None

    for i in range(DEPTH):
        w_in = w_in0 if i == 0 else w_in_rest[i - 1]
        zcols = lambda n: jnp.zeros((d, n), w_in.dtype)
        w_att = jnp.concatenate([w_in[:, :N_ATT], zcols(N_ATT_PAD - N_ATT)], axis=1)
        n_rw = w_in.shape[1] - (N_ATT + N_GATE)
        w_rw = jnp.concatenate([w_in[:, N_ATT + N_GATE:], zcols(N_RW_PAD - n_rw)], axis=1)
        mu_parts = [mu_rwkv[i]] + ([mu_vres[i - 1]] if i > 0 else [])
        mu_rw = jnp.concatenate(mu_parts + [jnp.zeros((N_RW_PAD - n_rw,), jnp.float32)])
        gates = _mm_gate(x.reshape(m, d), w_in[:, N_ATT:N_ATT + N_GATE], b_gate[i]).reshape(bsz, seq, N_GATE)
        p_rw = _mm_shift(x, w_rw, mu_rw)

        o = _attention(*_att_prep(x, w_att, g_cq[i], w_uq[i], w_iq[i], g_ik[i], b_ik[i], tabs_a, tabs_i), topk)

        f2 = lambda t_: t_.reshape(m, t_.shape[-1])
        f3 = lambda t_: t_.reshape(bsz, seq, t_.shape[-1])
        v_col = 2
        if i == 0:
            v_first = p_rw
            lw, k_r, kk_n, b_r, g_r = _rw_prep(f2(p_rw), None, w0[i], a0[i], None, k_k[i], k_a[i],
                                               w2[i], a2[i], g2[i], None)
            v_r = p_rw
        else:
            lw, k_r, kk_n, b_r, g_r, v_r = _rw_prep(f2(p_rw), f2(v_first), w0[i], a0[i], v0[i - 1],
                                                    k_k[i], k_a[i], w2[i], a2[i], g2[i], v2[i - 1])
            v_r = f3(v_r)
            v_col = 0
        y = _rwkv_scan(p_rw, f3(lw), f3(k_r), v_r, f3(kk_n), f3(b_r), r_col=0, v_col=v_col)

        x = _mixer_post(f2(y), f2(p_rw), k_r, f2(v_r), g_r, f2(o), f2(gates), f2(x),
                        g_lnx[i], b_lnx[i], r_k[i], ln1_g[i], ln1_b[i],
                        w_ob[i], w_oa[i], w_out[i], alpha, v_col=v_col).reshape(bsz, seq, d)

        x = _ffn(x, w_up[i], conv_w[i], conv_b[i], w_down[i], ln2_g[i], ln2_b[i], alpha)
    return x
```

```python
import functools

import jax
import jax.numpy as jnp
import numpy as np
from jax import lax
from jax.experimental import pallas as pl
from jax.experimental.pallas import tpu as pltpu

D_MODEL = 1024
DEPTH = 4
H_A = 8
DH_A = 128
D_CQ = 256
ROT_A = DH_A // 4
H_IDX = 8
D_IDX = 64
ROT_IDX = D_IDX // 4
TOPK_MAX = 256
ROPE_THETA = 500000.0
N_R = 64
H_R = D_MODEL // N_R
D_R = H_R * N_R
LORA_W = 64
LORA_A = 64
LORA_V = 32
LORA_G = 128
GN_EPS = 64e-5
D_FF = ((8 * D_MODEL // 3 + 127) // 128) * 128
LN_EPS = 1e-5

N_ATT = D_CQ + 2 * DH_A + D_IDX + H_IDX
N_GATE = 2 * D_MODEL
N_RWKV = 3 * D_R + LORA_W + LORA_A + LORA_G
N_IN0 = N_ATT + N_GATE + N_RWKV

LANES = 128
SUBLANES = 8
LOG2E = 1.4426950408889634
VMEM_LIMIT_BYTES = 48 * 1024 * 1024

N_ATT_PAD = 640
N_RW_PAD = 3584

ATTN_TILE = 256
SCAN_CHUNK = 64
SCAN_GROUP_TILES = 1
NEG_BIG = -1e30
INT_MIN = -(2 ** 31)


def _pick_tile(n, cap):
    best = None
    for t in range(LANES, min(n, cap) + 1, LANES):
        if n % t == 0:
            best = t
    return best if best is not None else n


def _gate_kernel(x_ref, w_ref, b_ref, o_ref):
    rows = o_ref.shape[0] // 2
    ps = [jnp.dot(x_ref[i * rows:(i + 1) * rows, :].astype(jnp.bfloat16), w_ref[...],
                  preferred_element_type=jnp.float32) for i in range(2)]
    for i in range(2):
        o_ref[i * rows:(i + 1) * rows, :] = jax.nn.sigmoid(ps[i] + b_ref[...])


def _mm_gate(x, w, bias):
    m, k = x.shape
    n = w.shape[1]
    tm = min(1024, m)
    tn = _pick_tile(n, 1024)
    return pl.pallas_call(
        _gate_kernel,
        grid=(m // tm, n // tn),
        in_specs=[pl.BlockSpec((tm, k), lambda i, j: (i, 0)),
                  pl.BlockSpec((k, tn), lambda i, j: (0, j)),
                  pl.BlockSpec((1, tn), lambda i, j: (0, j))],
        out_specs=pl.BlockSpec((tm, tn), lambda i, j: (i, j)),
        out_shape=jax.ShapeDtypeStruct((m, n), jnp.float32),
        compiler_params=pltpu.CompilerParams(
            dimension_semantics=("parallel", "arbitrary"),
            vmem_limit_bytes=VMEM_LIMIT_BYTES),
        name="mm_gate",
    )(x, w.astype(jnp.bfloat16), bias.reshape(1, n))


def _shift_rows(cur, prev_rows, k):
    rolled = pltpu.roll(cur, k, 0)
    row = lax.broadcasted_iota(jnp.int32, cur.shape, 0)
    for i in range(k):
        rolled = jnp.where(row == i, prev_rows[i:i + 1, :], rolled)
    return rolled


def _shift_mm_kernel(x_ref, w_ref, mu_ref, o_ref, carry_ref):
    si = pl.program_id(1)
    j = pl.program_id(2)
    tm = o_ref.shape[1]

    @pl.when(si == 0)
    def _():
        carry_ref[j] = jnp.zeros(carry_ref.shape[1:], jnp.float32)

    rows = tm // 2
    ps = [jnp.dot(x_ref[0, i * rows:(i + 1) * rows, :].astype(jnp.bfloat16), w_ref[...],
                  preferred_element_type=jnp.float32) for i in range(2)]
    last = carry_ref[j, 0:1, :]
    carry_ref[j, 0:1, :] = ps[1][rows - 1:rows, :]
    for i in range(2):
        p = ps[i]
        prev = _shift_rows(p, last, 1)
        last = p[rows - 1:rows, :]
        o_ref[0, i * rows:(i + 1) * rows, :] = p + (prev - p) * mu_ref[...]


def _mm_shift(x, w, mu):
    b, s, k = x.shape
    n = w.shape[1]
    tm = min(1024, s)
    tn = 512 if n % 512 == 0 else _pick_tile(n, 1024)
    return pl.pallas_call(
        _shift_mm_kernel,
        grid=(b, s // tm, n // tn),
        in_specs=[pl.BlockSpec((1, tm, k), lambda bi, si, j: (bi, si, 0)),
                  pl.BlockSpec((k, tn), lambda bi, si, j: (0, j)),
                  pl.BlockSpec((1, tn), lambda bi, si, j: (0, j))],
        out_specs=pl.BlockSpec((1, tm, tn), lambda bi, si, j: (bi, si, j)),
        out_shape=jax.ShapeDtypeStruct((b, s, n), jnp.float32),
        scratch_shapes=[pltpu.VMEM((n // tn, 8, tn), jnp.float32)],
        compiler_params=pltpu.CompilerParams(
            dimension_semantics=("parallel", "arbitrary", "arbitrary"),
            vmem_limit_bytes=VMEM_LIMIT_BYTES),
        name="mm_shift",
    )(x, w.astype(jnp.bfloat16), mu.reshape(1, n))


def _ffn_kernel(x_ref, wg_ref, wv_ref, cw_ref, cb_ref, wd_ref, g_ref, b_ref, o_ref,
                xb_ref, acc_ref, carry_ref, *, alpha, eps):
    si = pl.program_id(1)
    c = pl.program_id(2)
    tm = o_ref.shape[1]

    @pl.when(c == 0)
    def _():
        xb_ref[...] = x_ref[0].astype(jnp.bfloat16)
        acc_ref[...] = jnp.zeros(acc_ref.shape, jnp.float32)

    @pl.when(si == 0)
    def _():
        carry_ref[c] = jnp.zeros(carry_ref.shape[1:], jnp.float32)

    nsub = 2
    rows = tm // nsub
    hg = [jnp.dot(xb_ref[i * rows:(i + 1) * rows, :], wg_ref[...], preferred_element_type=jnp.float32)
          for i in range(nsub)]
    hv = [jnp.dot(xb_ref[i * rows:(i + 1) * rows, :], wv_ref[...], preferred_element_type=jnp.float32)
          for i in range(nsub)]
    tail = carry_ref[c, 0:2, :]
    carry_ref[c, 0:2, :] = hg[nsub - 1][rows - 2:rows, :]
    cw = cw_ref[...]
    for i in range(nsub):
        g1 = _shift_rows(hg[i], tail[1:2, :], 1)
        g2 = _shift_rows(hg[i], tail, 2)
        tail = hg[i][rows - 2:rows, :]
        conv = g2 * cw[0:1, :] + g1 * cw[1:2, :] + hg[i] * cw[2:3, :] + cb_ref[...]
        act = jax.nn.silu(conv) * hv[i]
        acc_ref[i * rows:(i + 1) * rows, :] += jnp.dot(act.astype(jnp.bfloat16), wd_ref[...],
                                                       preferred_element_type=jnp.float32)

    @pl.when(c == pl.num_programs(2) - 1)
    def _():
        y = alpha * x_ref[0] + acc_ref[...]
        mu = jnp.mean(y, axis=-1, keepdims=True)
        d = y - mu
        var = jnp.mean(d * d, axis=-1, keepdims=True)
        o_ref[0] = d * lax.rsqrt(var + eps) * g_ref[...] + b_ref[...]


def _ffn(x, w_up, conv_w, conv_b, w_down, ln_g, ln_b, alpha):
    b, s, d = x.shape
    dff = w_down.shape[0]
    tm = min(1024, s)
    tf = 256 if dff % 256 == 0 else LANES
    nc = dff // tf
    kern = functools.partial(_ffn_kernel, alpha=alpha, eps=LN_EPS)
    w_up = w_up.astype(jnp.bfloat16)
    return pl.pallas_call(
        kern,
        grid=(b, s // tm, nc),
        in_specs=[
            pl.BlockSpec((1, tm, d), lambda bi, si, c: (bi, si, 0)),
            pl.BlockSpec((d, tf), lambda bi, si, c: (0, c)),
            pl.BlockSpec((d, tf), lambda bi, si, c: (0, c + nc)),
            pl.BlockSpec((3, tf), lambda bi, si, c: (0, c)),
            pl.BlockSpec((1, tf), lambda bi, si, c: (0, c)),
            pl.BlockSpec((tf, d), lambda bi, si, c: (c, 0)),
            pl.BlockSpec((1, d), lambda bi, si, c: (0, 0)),
            pl.BlockSpec((1, d), lambda bi, si, c: (0, 0)),
        ],
        out_specs=pl.BlockSpec((1, tm, d), lambda bi, si, c: (bi, si, 0)),
        out_shape=jax.ShapeDtypeStruct((b, s, d), jnp.float32),
        scratch_shapes=[
            pltpu.VMEM((tm, d), jnp.bfloat16),
            pltpu.VMEM((tm, d), jnp.float32),
            pltpu.VMEM((nc, 8, tf), jnp.float32),
        ],
        compiler_params=pltpu.CompilerParams(
            dimension_semantics=("parallel", "arbitrary", "arbitrary"),
            vmem_limit_bytes=VMEM_LIMIT_BYTES),
        name="ffn",
    )(x, w_up, w_up, conv_w, conv_b.reshape(1, dff), w_down.astype(jnp.bfloat16),
      ln_g.reshape(1, d), ln_b.reshape(1, d))


def _sortable_key(score):
    score = jnp.where(score == 0.0, 0.0, score)
    bits = pltpu.bitcast(score, jnp.int32)
    return jnp.where(bits < 0, bits ^ jnp.int32(0x7FFFFFFF), bits)


def _attn_kernel(qt_ref, qit_ref, wi_ref, k_ref, vt_ref, ki_ref, o_ref,
                 key_ref, jthr_ref, m_ref, acc_ref, *, t, nh, topk):
    qb = pl.program_id(1)
    n_kt = qb + 1
    krow = lax.broadcasted_iota(jnp.int32, (t, t), 0)
    qcol = lax.broadcasted_iota(jnp.int32, (t, t), 1)

    qit = qit_ref[0, 0]
    wi = wi_ref[0, 0]

    def score_tile(kt, carry):
        ki_t = ki_ref[0, pl.ds(kt * t, t), :]
        rel = jnp.dot(ki_t, qit, preferred_element_type=jnp.float32)
        rel = jnp.maximum(rel, 0.0) * wi
        score = rel[:, 0:t]
        for h in range(1, nh):
            score = score + rel[:, h * t:(h + 1) * t]
        score = jnp.where(jnp.logical_and(kt == qb, krow > qcol), -jnp.inf, score)
        key_ref[kt] = _sortable_key(score)
        return carry

    lax.fori_loop(0, n_kt, score_tile, 0)

    def count(pred_fn):
        def body(kt, accs):
            hit = pred_fn(key_ref[kt], kt)
            accs = list(accs)
            for j in range(t // 8):
                a = accs[j % len(accs)]
                accs[j % len(accs)] = jnp.where(hit[j * 8:(j + 1) * 8], a + 1, a)
            return tuple(accs)
        accs = lax.fori_loop(0, n_kt, body, tuple(jnp.zeros((8, t), jnp.int32) for _ in range(4)))
        return jnp.sum(accs[0] + accs[1] + accs[2] + accs[3], axis=0, keepdims=True)

    def count_ge(cand):
        return count(lambda kk, kt: kk >= cand)

    cnt0 = count_ge(jnp.zeros((1, t), jnp.int32))
    nonneg = cnt0 >= topk
    thr0 = jnp.where(nonneg, jnp.int32(0), jnp.int32(INT_MIN))
    n_ge0 = jnp.where(nonneg, cnt0, n_kt * t)

    def bit_step(i, state):
        thr, n_ge = state
        cand = thr | jnp.left_shift(jnp.int32(1), 30 - i)
        cnt = count_ge(cand)
        take = cnt >= topk
        return jnp.where(take, cand, thr), jnp.where(take, cnt, n_ge)

    thr, n_ge = lax.fori_loop(0, 31, bit_step, (thr0, n_ge0))
    jthr_ref[...] = jnp.full(jthr_ref.shape, 2 ** 30, jnp.int32)

    @pl.when(jnp.max(n_ge) > topk)
    def _():
        need = topk - count(lambda kk, kt: kk > thr)
        idx_bits = int(np.ceil(np.log2(key_ref.shape[0] * t)))

        def idx_step(i, j):
            cand = j | jnp.left_shift(jnp.int32(1), idx_bits - 1 - i)
            below = count(lambda kk, kt: jnp.logical_and(kk == thr, krow + kt * t < cand))
            return jnp.where(below < need, cand, j)

        jthr_ref[...] = lax.fori_loop(0, idx_bits, idx_step, jnp.zeros((1, t), jnp.int32))

    qt = qt_ref[0, 0]
    m_ref[...] = jnp.full(m_ref.shape, NEG_BIG, jnp.float32)
    acc_ref[...] = jnp.zeros(acc_ref.shape, jnp.float32)
    jthr = jthr_ref[...]

    def attn_tile(kt, carry):
        kk = key_ref[kt]
        gk = krow + kt * t
        sel = jnp.logical_or(kk > thr, jnp.logical_and(kk == thr, gk <= jthr))
        sel = jnp.logical_and(sel, gk <= qcol + qb * t)
        bias = jnp.where(sel, 0.0, NEG_BIG)
        k_t = k_ref[0, pl.ds(kt * t, t), :]
        vt_t = vt_ref[0, kt]
        s_all = jnp.dot(k_t, qt, preferred_element_type=jnp.float32)
        for h in range(nh):
            s = s_all[:, h * t:(h + 1) * t] + bias
            m_old = m_ref[h:h + 1, :]
            m_new = jnp.maximum(m_old, jnp.max(s, axis=0, keepdims=True))
            alpha = jnp.exp2(m_old - m_new)
            p = jnp.exp2(s - m_new)
            acc_ref[h] = acc_ref[h] * alpha + jnp.dot(
                vt_t, p.astype(jnp.bfloat16), preferred_element_type=jnp.float32)
            m_ref[h:h + 1, :] = m_new
        return carry

    lax.fori_loop(0, n_kt, attn_tile, 0)
    dh = o_ref.shape[2] // nh
    for h in range(nh):
        acc = acc_ref[h]
        out = acc[:dh] / acc[dh:dh + 1]
        o_ref[0, :, h * dh:(h + 1) * dh] = out.T.astype(o_ref.dtype)


def _rope_rows(blk, cos, sin, half):
    x1, x2 = blk[:half], blk[half:2 * half]
    return jnp.concatenate([x1 * cos - x2 * sin, x2 * cos + x1 * sin, blk[2 * half:]], axis=0)


def _rope_lanes(x, c, sa, sb, half):
    return x * c + pltpu.roll(x, LANES - half, 1) * sa + pltpu.roll(x, half, 1) * sb


def _attprep_kernel(x_ref, watt_ref, gcq_ref, wuqt_ref, wiqt_ref, gik_ref, bik_ref,
                    cosa_ref, sina_ref, cosi_ref, sini_ref,
                    kc_ref, ksa_ref, ksb_ref, ic_ref, isa_ref, isb_ref,
                    qt_ref, qit_ref, wit_ref, k_ref, vt_ref, ki_ref, *, t, nh, q_scale, w_scale):
    bf = jnp.bfloat16
    nt = (((1,), (1,)), ((), ()))
    p = jnp.dot(x_ref[0].astype(bf), watt_ref[...], preferred_element_type=jnp.float32)
    cq = p[:, :D_CQ]
    c_q = cq * lax.rsqrt(jnp.mean(cq * cq, axis=-1, keepdims=True) + 1e-6) * gcq_ref[...]
    cqb = c_q.astype(bf)

    q_t = lax.dot_general(wuqt_ref[...], cqb, nt, preferred_element_type=jnp.float32)
    cos, sin = cosa_ref[0], sina_ref[0]
    for h in range(nh):
        blk = _rope_rows(q_t[h * DH_A:(h + 1) * DH_A], cos, sin, ROT_A // 2) * q_scale
        qt_ref[0, 0, :, h * t:(h + 1) * t] = blk.astype(bf)
    qi_t = lax.dot_general(wiqt_ref[...], cqb, nt, preferred_element_type=jnp.float32)
    cos, sin = cosi_ref[0], sini_ref[0]
    for h in range(nh):
        blk = _rope_rows(qi_t[h * D_IDX:(h + 1) * D_IDX], cos, sin, ROT_IDX // 2)
        qit_ref[0, 0, :, h * t:(h + 1) * t] = blk.astype(bf)

    k_raw = p[:, D_CQ:D_CQ + DH_A]
    k_ref[0] = _rope_lanes(k_raw, kc_ref[0], ksa_ref[0], ksb_ref[0], ROT_A // 2).astype(bf)
    v_raw = p[:, D_CQ + DH_A:D_CQ + 2 * DH_A]
    vt_ref[0, 0, :DH_A, :] = v_raw.T.astype(bf)
    vt_ref[0, 0, DH_A:, :] = jnp.ones((vt_ref.shape[2] - DH_A, t), bf)

    tail = p[:, D_CQ + 2 * DH_A:]
    lane = lax.broadcasted_iota(jnp.int32, (1, LANES), 1)
    is_key = lane < D_IDX
    mu = jnp.sum(jnp.where(is_key, tail, 0.0), axis=-1, keepdims=True) * (1.0 / D_IDX)
    dk = jnp.where(is_key, tail - mu, 0.0)
    var = jnp.sum(dk * dk, axis=-1, keepdims=True) * (1.0 / D_IDX)
    ln = dk * lax.rsqrt(var + LN_EPS) * gik_ref[...] + bik_ref[...]
    ki = _rope_lanes(ln, ic_ref[0], isa_ref[0], isb_ref[0], ROT_IDX // 2)
    ki_ref[0] = ki[:, :D_IDX].astype(bf)
    w_t = tail.T[D_IDX:D_IDX + nh] * w_scale
    for h in range(nh):
        wit_ref[0, 0, :, h * t:(h + 1) * t] = w_t[h:h + 1]


def _rope_tables_all(positions):
    def tables(rot):
        half = rot // 2
        inv_freq = jnp.power(jnp.float32(ROPE_THETA), -jnp.arange(0, rot, 2, dtype=jnp.float32) / rot)
        ang = positions.astype(jnp.float32)[..., None] * inv_freq
        cos, sin = jnp.cos(ang), jnp.sin(ang)
        ones = jnp.ones(ang.shape[:2] + (LANES - 2 * half,), jnp.float32)
        zeros = lambda n: jnp.zeros(ang.shape[:2] + (n,), jnp.float32)
        c = jnp.concatenate([cos, cos, ones], axis=-1)
        sa = jnp.concatenate([-sin, zeros(LANES - half)], axis=-1)
        sb = jnp.concatenate([zeros(half), sin, zeros(LANES - 2 * half)], axis=-1)
        return cos.swapaxes(1, 2), sin.swapaxes(1, 2), c, sa, sb
    return tables(ROT_A), tables(ROT_IDX)


def _att_prep(x, w_att, g_cq, w_uq, w_iq, g_ik, b_ik, tabs_a, tabs_i):
    b, s, d = x.shape
    t = min(ATTN_TILE, s)
    nq = s // t
    nh = H_A
    bf = jnp.bfloat16
    dhe = DH_A + SUBLANES
    pad_vec = lambda vv: jnp.concatenate([vv, jnp.zeros((LANES - vv.shape[0],), vv.dtype)]).reshape(1, LANES)
    cos_a, sin_a, kc, ksa, ksb = tabs_a
    cos_i, sin_i, ic, isa, isb = tabs_i
    const = lambda a, b_: pl.BlockSpec((a, b_), lambda bi, qb: (0, 0))
    fm = lambda rows: pl.BlockSpec((1, rows, t), lambda bi, qb: (bi, 0, qb))
    tm_ = pl.BlockSpec((1, t, LANES), lambda bi, qb: (bi, qb, 0))
    kern = functools.partial(_attprep_kernel, t=t, nh=nh,
                             q_scale=float(DH_A) ** -0.5 * LOG2E, w_scale=H_IDX ** -0.5 * D_IDX ** -0.5)
    return pl.pallas_call(
        kern,
        grid=(b, nq),
        in_specs=[
            pl.BlockSpec((1, t, d), lambda bi, qb: (bi, qb, 0)),
            const(d, N_ATT_PAD), const(1, D_CQ), const(nh * DH_A, D_CQ), const(nh * D_IDX, D_CQ),
            const(1, LANES), const(1, LANES),
            fm(ROT_A // 2), fm(ROT_A // 2), fm(ROT_IDX // 2), fm(ROT_IDX // 2),
            tm_, tm_, tm_, tm_, tm_, tm_,
        ],
        out_specs=[
            pl.BlockSpec((1, 1, DH_A, nh * t), lambda bi, qb: (bi, qb, 0, 0)),
            pl.BlockSpec((1, 1, D_IDX, nh * t), lambda bi, qb: (bi, qb, 0, 0)),
            pl.BlockSpec((1, 1, 1, nh * t), lambda bi, qb: (bi, qb, 0, 0)),
            pl.BlockSpec((1, t, DH_A), lambda bi, qb: (bi, qb, 0)),
            pl.BlockSpec((1, 1, dhe, t), lambda bi, qb: (bi, qb, 0, 0)),
            pl.BlockSpec((1, t, D_IDX), lambda bi, qb: (bi, qb, 0)),
        ],
        out_shape=[
            jax.ShapeDtypeStruct((b, nq, DH_A, nh * t), bf),
            jax.ShapeDtypeStruct((b, nq, D_IDX, nh * t), bf),
            jax.ShapeDtypeStruct((b, nq, 1, nh * t), jnp.float32),
            jax.ShapeDtypeStruct((b, s, DH_A), bf),
            jax.ShapeDtypeStruct((b, nq, dhe, t), bf),
            jax.ShapeDtypeStruct((b, s, D_IDX), bf),
        ],
        compiler_params=pltpu.CompilerParams(
            dimension_semantics=("parallel", "arbitrary"),
            vmem_limit_bytes=VMEM_LIMIT_BYTES),
        name="att_prep",
    )(x, w_att.astype(bf), g_cq.reshape(1, D_CQ), w_uq.T.astype(bf), w_iq.T.astype(bf),
      pad_vec(g_ik), pad_vec(b_ik), cos_a, sin_a, cos_i, sin_i, kc, ksa, ksb, ic, isa, isb)


def _attention(qt, qit, wit, k, vt, ki, topk):
    b, nq, dh, nht = qt.shape
    di = qit.shape[2]
    s = k.shape[1]
    t = s // nq
    nh = nht // t
    dhe = vt.shape[2]
    kern = functools.partial(_attn_kernel, t=t, nh=nh, topk=topk)
    return pl.pallas_call(
        kern,
        grid=(b, nq),
        in_specs=[
            pl.BlockSpec((1, 1, dh, nh * t), lambda bi, qb: (bi, qb, 0, 0)),
            pl.BlockSpec((1, 1, di, nh * t), lambda bi, qb: (bi, qb, 0, 0)),
            pl.BlockSpec((1, 1, 1, nh * t), lambda bi, qb: (bi, qb, 0, 0)),
            pl.BlockSpec((1, s, dh), lambda bi, qb: (bi, 0, 0)),
            pl.BlockSpec((1, nq, dhe, t), lambda bi, qb: (bi, 0, 0, 0)),
            pl.BlockSpec((1, s, di), lambda bi, qb: (bi, 0, 0)),
        ],
        out_specs=pl.BlockSpec((1, t, nh * dh), lambda bi, qb: (bi, qb, 0)),
        out_shape=jax.ShapeDtypeStruct((b, s, nh * dh), jnp.bfloat16),
        scratch_shapes=[
            pltpu.VMEM((nq, t, t), jnp.int32),
            pltpu.VMEM((1, t), jnp.int32),
            pltpu.VMEM((nh, t), jnp.float32),
            pltpu.VMEM((nh, dhe, t), jnp.float32),
        ],
        compiler_params=pltpu.CompilerParams(
            dimension_semantics=("parallel", "arbitrary"),
            vmem_limit_bytes=VMEM_LIMIT_BYTES),
        name="dsa_attention",
    )(qt, qit, wit, k, vt, ki)


def _bdot(a, b, dims=(((1,), (0,)), ((), ()))):
    return lax.dot_general(a.astype(jnp.bfloat16), b.astype(jnp.bfloat16), dims,
                           preferred_element_type=jnp.float32)


_NT = (((1,), (1,)), ((), ()))
_TN = (((0,), (0,)), ((), ()))


def _scan_kernel(r_ref, lw_ref, k_ref, v_ref, kk_ref, b_ref, y_ref, s_ref, *, chunk):
    @pl.when(pl.program_id(1) == 0)
    def _():
        s_ref[...] = jnp.zeros(s_ref.shape, jnp.float32)

    n_tiles = s_ref.shape[0]
    hl = LANES // 2
    gt = SCAN_GROUP_TILES
    rows = 2 * gt * chunk
    row = lax.broadcasted_iota(jnp.int32, (rows, rows), 0)
    col = lax.broadcasted_iota(jnp.int32, (rows, rows), 1)
    same = (row // chunk) == (col // chunk)
    incl = jnp.logical_and(same, row >= col)
    strict = jnp.logical_and(same, row > col)
    r1 = lax.broadcasted_iota(jnp.int32, (chunk, chunk), 0)
    c1 = lax.broadcasted_iota(jnp.int32, (chunk, chunk), 1)
    tril = jnp.where(r1 >= c1, 1.0, 0.0).astype(jnp.float32)
    lane = lax.broadcasted_iota(jnp.int32, (1, LANES), 1)
    even = lane < hl
    pr_ = lax.broadcasted_iota(jnp.int32, (LANES, LANES), 0)
    pc_ = lax.broadcasted_iota(jnp.int32, (LANES, LANES), 1)
    pair_bd = (pr_ // hl) == (pc_ // hl)

    lw = lw_ref[0]
    c = jnp.dot(tril, lw, precision=lax.Precision.HIGHEST,
                preferred_element_type=jnp.float32)
    e_neg = jnp.exp(-c)
    at = -kk_ref[0] * jnp.exp(c - lw)
    rt = r_ref[0] * jnp.exp(c)
    bt = b_ref[0] * e_neg
    kt = k_ref[0] * e_neg
    v = v_ref[0]
    p_last = jnp.exp(c[chunk - 1:chunk, :])
    bp = bt * p_last
    kp = kt * p_last

    def tile(x, t):
        return x[:, t * LANES:(t + 1) * LANES]

    def stack_masked(x, t0):
        parts = []
        for t in range(t0, t0 + gt):
            xt = tile(x, t)
            parts += [jnp.where(even, xt, 0.0), jnp.where(even, 0.0, xt)]
        return jnp.concatenate(parts, axis=0)

    def stack_plain(x, t0):
        return jnp.concatenate([tile(x, t) for t in range(t0, t0 + gt) for _ in range(2)], axis=0)

    steps = max(1, int(np.ceil(np.log2(chunk))))
    groups = list(range(0, n_tiles, gt))
    at_s = [stack_masked(at, t0) for t0 in groups]
    v_s = [stack_masked(v, t0) for t0 in groups]
    a_all = [_bdot(jnp.concatenate([at_s[g], stack_masked(rt, t0)], axis=0),
                   jnp.concatenate([stack_plain(bt, t0), stack_plain(kt, t0)], axis=0), _NT)
             for g, t0 in enumerate(groups)]
    mpow = [jnp.where(strict, a[:rows, :rows], 0.0) for a in a_all]
    a_ak = [jnp.where(strict, a[:rows, rows:], 0.0) for a in a_all]
    a_rb = [jnp.where(incl, a[rows:, :rows], 0.0) for a in a_all]
    a_rk = [jnp.where(incl, a[rows:, rows:], 0.0) for a in a_all]

    xs = [jnp.concatenate([at_s[g], _bdot(a_ak[g], v_s[g])], axis=1) for g in range(len(groups))]
    for i in range(steps):
        xs = [x + _bdot(m_, x) for x, m_ in zip(xs, mpow)]
        if i + 1 < steps:
            mpow = [_bdot(m_, m_) for m_ in mpow]
    qys = [_bdot(a_rb[g], xs[g]) for g in range(len(groups))]
    y0s = [qys[g][:, LANES:] + _bdot(a_rk[g], v_s[g]) for g in range(len(groups))]

    for g, t0 in enumerate(groups):
        x = xs[g]
        y0_s = y0s[g]
        qh_s = qys[g][:, :LANES]
        for j in range(gt):
            t = t0 + j
            lo, mid, hi = 2 * j * chunk, (2 * j + 1) * chunk, (2 * j + 2) * chunk
            unstack = lambda z: z[lo:mid] + z[mid:hi]
            x_t = unstack(x)
            qh_t = tile(rt, t) + unstack(qh_s)
            y0_t = unstack(y0_s)
            s0 = s_ref[t]
            gw = _bdot(x_t, tile(bp, t), _TN)
            g_t = jnp.where(pair_bd, gw[:LANES], 0.0)
            h_t = jnp.where(pair_bd, gw[LANES:] + _bdot(tile(v, t), tile(kp, t), _TN), 0.0)
            y_ref[0, :, t * LANES:(t + 1) * LANES] = y0_t + _bdot(qh_t, s0, _NT)
            s_ref[t] = s0 * tile(p_last, t) + _bdot(s0, g_t) + h_t


def _rwkv_scan(r, lw, k, v, kk, b, r_col=0, v_col=0):
    bsz, s, _ = lw.shape
    d = D_R
    chunk = min(SCAN_CHUNK, s)
    spec = pl.BlockSpec((1, chunk, d), lambda bi, ci: (bi, ci, 0))
    colspec = lambda j: pl.BlockSpec((1, chunk, d), lambda bi, ci, j=j: (bi, ci, j))
    kern = functools.partial(_scan_kernel, chunk=chunk)
    return pl.pallas_call(
        kern,
        grid=(bsz, s // chunk),
        in_specs=[colspec(r_col), spec, spec, colspec(v_col), spec, spec],
        out_specs=spec,
        out_shape=jax.ShapeDtypeStruct((bsz, s, d), jnp.float32),
        scratch_shapes=[pltpu.VMEM((d // LANES, LANES, LANES), jnp.float32)],
        compiler_params=pltpu.CompilerParams(
            dimension_semantics=("parallel", "arbitrary"),
            vmem_limit_bytes=VMEM_LIMIT_BYTES),
        name="rwkv7_scan",
    )(r, lw, k, v, kk, b)


RW_COL_LORA = 3 * D_R // LANES
RW_COL_GATE = RW_COL_LORA + 1
RW_COL_VRES = RW_COL_LORA + 2


def _rwprep_kernel(*refs, has_vres):
    if has_vres:
        (pk_ref, pv_ref, pwa_ref, pg_ref, pvr_ref, vf_ref, w0_ref, a0_ref, v0_ref, kk_ref, ka_ref,
         w2_ref, a2_ref, g2_ref, v2_ref, seg_ref, lw_ref, k_out, kkn_ref, b_ref, g_out, v_out) = refs
    else:
        (pk_ref, pwa_ref, pg_ref, w0_ref, a0_ref, kk_ref, ka_ref,
         w2_ref, a2_ref, g2_ref, seg_ref, lw_ref, k_out, kkn_ref, b_ref, g_out) = refs
    bf = jnp.bfloat16
    dotf = lambda a, w: jnp.dot(a.astype(bf), w, preferred_element_type=jnp.float32)
    pwa = pwa_ref[...]
    u = w0_ref[...] + dotf(jnp.tanh(pwa), w2_ref[...])
    neg = -u
    softplus = jnp.maximum(neg, 0.0) + jnp.log(1.0 + jnp.exp(-jnp.abs(neg)))
    lw_ref[...] = -jnp.exp(-softplus - 0.5)
    a = jax.nn.sigmoid(a0_ref[...] + dotf(pwa, a2_ref[...]))
    g_out[...] = dotf(jax.nn.sigmoid(pg_ref[...]), g2_ref[...])
    pk = pk_ref[...]
    kk = pk * kk_ref[...]
    norm = jnp.sqrt(_head_sum(kk * kk, seg_ref[...]))
    kk = kk / jnp.maximum(norm, 1e-12)
    kkn_ref[...] = kk
    b_ref[...] = kk * a
    k_out[...] = pk * (1.0 + (a - 1.0) * ka_ref[...])
    if has_vres:
        pv = pv_ref[...]
        mix = jax.nn.sigmoid(v0_ref[...] + dotf(pvr_ref[...], v2_ref[...]))
        v_out[...] = pv + (vf_ref[...] - pv) * mix


def _rw_prep(p_rw, v_first, w0, a0, v0, k_k, k_a, w2, a2, g2, v2):
    m = p_rw.shape[0]
    d = D_R
    tm = min(256, m)
    has_vres = v_first is not None
    bf = jnp.bfloat16
    hl = LANES // 2
    seg = (jnp.arange(LANES)[:, None] // hl == jnp.arange(LANES)[None, :] // hl).astype(bf)
    pad_rows = lambda w, top: jnp.concatenate(
        [jnp.zeros((top, d), w.dtype), w, jnp.zeros((LANES - top - w.shape[0], d), w.dtype)], axis=0).astype(bf)
    col = lambda j, width: pl.BlockSpec((tm, width), lambda i, j=j: (i, j))
    vec = pl.BlockSpec((1, d), lambda i: (0, 0))
    wspec = pl.BlockSpec((LANES, d), lambda i: (0, 0))
    segspec = pl.BlockSpec((LANES, LANES), lambda i: (0, 0))
    row_out = pl.BlockSpec((tm, d), lambda i: (i, 0))
    v1 = lambda t_: t_.reshape(1, d)
    if has_vres:
        args = [p_rw, p_rw, p_rw, p_rw, p_rw, v_first, v1(w0), v1(a0), v1(v0), v1(k_k), v1(k_a),
                pad_rows(w2, 0), pad_rows(a2, LORA_W), g2.astype(bf), pad_rows(v2, 0), seg]
        in_specs = [col(1, d), col(2, d), col(RW_COL_LORA, LANES), col(RW_COL_GATE, LANES),
                    col(RW_COL_VRES, LANES), col(2, d), vec, vec, vec, vec, vec,
                    wspec, wspec, wspec, wspec, segspec]
        n_out = 6
    else:
        args = [p_rw, p_rw, p_rw, v1(w0), v1(a0), v1(k_k), v1(k_a),
                pad_rows(w2, 0), pad_rows(a2, LORA_W), g2.astype(bf), seg]
        in_specs = [col(1, d), col(RW_COL_LORA, LANES), col(RW_COL_GATE, LANES), vec, vec, vec, vec,
                    wspec, wspec, wspec, segspec]
        n_out = 5
    return pl.pallas_call(
        functools.partial(_rwprep_kernel, has_vres=has_vres),
        grid=(m // tm,),
        in_specs=in_specs,
        out_specs=[row_out] * n_out,
        out_shape=[jax.ShapeDtypeStruct((m, d), jnp.float32)] * n_out,
        compiler_params=pltpu.CompilerParams(
            dimension_semantics=("parallel",),
            vmem_limit_bytes=VMEM_LIMIT_BYTES),
        name="rw_prep",
    )(*args)


def _head_sum(z, seg):
    hi = z.astype(jnp.bfloat16)
    lo = (z - hi.astype(jnp.float32)).astype(jnp.bfloat16)
    outs = []
    for t in range(z.shape[1] // LANES):
        sl = slice(t * LANES, (t + 1) * LANES)
        outs.append(jnp.dot(hi[:, sl], seg, preferred_element_type=jnp.float32)
                    + jnp.dot(lo[:, sl], seg, preferred_element_type=jnp.float32))
    return jnp.concatenate(outs, axis=1)


def _post_kernel(y_ref, r_ref, k_ref, v_ref, g_ref, o_ref, gate_ref, x_ref,
                 glnx_ref, blnx_ref, rk_ref, lng_ref, lnb_ref, seg_ref, wob_ref, woa_ref, wout_ref,
                 out_ref, *, alpha):
    seg = seg_ref[...]
    d_model = x_ref.shape[1]
    inv_n = 1.0 / N_R
    y = y_ref[...]
    dy = y - _head_sum(y, seg) * inv_n
    var = _head_sum(dy * dy, seg) * inv_n
    yn = dy * lax.rsqrt(var + GN_EPS) * glnx_ref[...] + blnx_ref[...]
    bonus = _head_sum(r_ref[...] * k_ref[...] * rk_ref[...], seg) * v_ref[...]
    yb_in = ((yn + bonus) * g_ref[...]).astype(jnp.bfloat16)
    y_b = jnp.dot(yb_in, wob_ref[...], preferred_element_type=jnp.float32)
    y_a = jnp.dot(o_ref[...], woa_ref[...], preferred_element_type=jnp.float32)
    gates = gate_ref[...]
    z = gates[:, :d_model] * y_a + gates[:, d_model:] * y_b
    t = alpha * x_ref[...] + jnp.dot(z.astype(jnp.bfloat16), wout_ref[...],
                                     preferred_element_type=jnp.float32)
    mu = jnp.mean(t, axis=-1, keepdims=True)
    dt = t - mu
    vt = jnp.mean(dt * dt, axis=-1, keepdims=True)
    out_ref[...] = dt * lax.rsqrt(vt + LN_EPS) * lng_ref[...] + lnb_ref[...]


def _mixer_post(y, r, k, v, g, o, gates, x, g_lnx, b_lnx, r_k, ln_g, ln_b, w_ob, w_oa, w_out, alpha, v_col=0):
    m, d = x.shape
    tm = min(256, m)
    hl = LANES // 2
    seg = (jnp.arange(LANES)[:, None] // hl == jnp.arange(LANES)[None, :] // hl).astype(jnp.bfloat16)
    row = lambda n, j=0: pl.BlockSpec((tm, n), lambda i: (i, j))
    vec = pl.BlockSpec((1, d), lambda i: (0, 0))
    full = lambda a, b_: pl.BlockSpec((a, b_), lambda i: (0, 0))
    bf = jnp.bfloat16
    return pl.pallas_call(
        functools.partial(_post_kernel, alpha=alpha),
        grid=(m // tm,),
        in_specs=[row(d), row(d), row(d), row(d, v_col), row(d), row(d), row(2 * d), row(d),
                  vec, vec, vec, vec, vec, full(LANES, LANES), full(d, d), full(d, d), full(d, d)],
        out_specs=row(d),
        out_shape=jax.ShapeDtypeStruct((m, d), jnp.float32),
        compiler_params=pltpu.CompilerParams(
            dimension_semantics=("parallel",),
            vmem_limit_bytes=VMEM_LIMIT_BYTES),
        name="mixer_post",
    )(y, r, k, v, g, o, gates, x,
      g_lnx.reshape(1, d), b_lnx.reshape(1, d), r_k.reshape(1, d), ln_g.reshape(1, d), ln_b.reshape(1, d),
      seg, w_ob.astype(bf), w_oa.astype(bf), w_out.astype(bf))


def kernel(x, positions, w_in0, w_in_rest, b_gate, g_cq, w_uq, w_iq, g_ik, b_ik, w_oa, mu_rwkv, mu_vres, w0, w2, a0, a2, v0, v2, g2, k_k, k_a, r_k, g_lnx, b_lnx, w_ob, w_out, ln1_g, ln1_b, w_up, conv_w, conv_b, w_down, ln2_g, ln2_b):
    bsz, seq, d = x.shape
    m = bsz * seq
    alpha = (2 * DEPTH) ** 0.25
    topk = min(TOPK_MAX, seq // 4)
    tabs_a, tabs_i = _rope_tables_all(positions)
    v_first = None

    for i in range(DEPTH):
        w_in = w_in0 if i == 0 else w_in_rest[i - 1]
        zcols = lambda n: jnp.zeros((d, n), w_in.dtype)
        w_att = jnp.concatenate([w_in[:, :N_ATT], zcols(N_ATT_PAD - N_ATT)], axis=1)
        n_rw = w_in.shape[1] - (N_ATT + N_GATE)
        w_rw = jnp.concatenate([w_in[:, N_ATT + N_GATE:], zcols(N_RW_PAD - n_rw)], axis=1)
        mu_parts = [mu_rwkv[i]] + ([mu_vres[i - 1]] if i > 0 else [])
        mu_rw = jnp.concatenate(mu_parts + [jnp.zeros((N_RW_PAD - n_rw,), jnp.float32)])
        gates = _mm_gate(x.reshape(m, d), w_in[:, N_ATT:N_ATT + N_GATE], b_gate[i]).reshape(bsz, seq, N_GATE)
        p_rw = _mm_shift(x, w_rw, mu_rw)

        o = _attention(*_att_prep(x, w_att, g_cq[i], w_uq[i], w_iq[i], g_ik[i], b_ik[i], tabs_a, tabs_i), topk)

        f2 = lambda t_: t_.reshape(m, t_.shape[-1])
        f3 = lambda t_: t_.reshape(bsz, seq, t_.shape[-1])
        v_col = 2
        if i == 0:
            v_first = p_rw
            lw, k_r, kk_n, b_r, g_r = _rw_prep(f2(p_rw), None, w0[i], a0[i], None, k_k[i], k_a[i],
                                               w2[i], a2[i], g2[i], None)
            v_r = p_rw
        else:
            lw, k_r, kk_n, b_r, g_r, v_r = _rw_prep(f2(p_rw), f2(v_first), w0[i], a0[i], v0[i - 1],
                                                    k_k[i], k_a[i], w2[i], a2[i], g2[i], v2[i - 1])
            v_r = f3(v_r)
            v_col = 0
        y = _rwkv_scan(p_rw, f3(lw), f3(k_r), v_r, f3(kk_n), f3(b_r), r_col=0, v_col=v_col)

        x = _mixer_post(f2(y), f2(p_rw), k_r, f2(v_r), g_r, f2(o), f2(gates), f2(x),
                        g_lnx[i], b_lnx[i], r_k[i], ln1_g[i], ln1_b[i],
                        w_ob[i], w_oa[i], w_out[i], alpha, v_col=v_col).reshape(bsz, seq, d)

        x = _ffn(x, w_up[i], conv_w[i], conv_b[i], w_down[i], ln2_g[i], ln2_b[i], alpha)
    return x
```
